```python
import functools
import jax, jax.numpy as jnp
from jax import lax
import numpy as np

D_MODEL = 1024
BATCH = 4
SEQ = 8192
DEPTH = 1
DEC_BATCH = 32
DEC_SEQ = 64
PAST_LEN = 2048

CHUNK = 64
N_META = 16
EPS = 1e-6
MLA_HEADS = 8
Q_LORA_RANK = 256
KV_LORA_RANK = 128
QK_NOPE_DIM = 64
ROPE_DIM = 32
V_HEAD_DIM = 64
ROPE_THETA = 10000.0
MLA_WIDTH = MLA_HEADS * V_HEAD_DIM
SM_SCALE = (QK_NOPE_DIM + ROPE_DIM) ** -0.5
Q_BLOCK = 128
HG_HEADS = 4
HG_KEY_DIM = 128
HG_VAL_DIM = 128
HG_KEY_WIDTH = HG_HEADS * HG_KEY_DIM
HG_WIDTH = HG_HEADS * HG_VAL_DIM
HG_BLOCK = 16
MIX_WIDTH = MLA_WIDTH + HG_WIDTH
IN_SPLITS = (Q_LORA_RANK, KV_LORA_RANK, ROPE_DIM, HG_KEY_WIDTH, HG_KEY_WIDTH, HG_WIDTH, HG_WIDTH)
IN_WIDTH = sum(IN_SPLITS)
N_GROUPS = 4
EXPERTS_PER_GROUP = 4
N_EXPERTS = N_GROUPS * EXPERTS_PER_GROUP
TOP_K = 2
EXPERT_FF = 256

kernel_name = 'hymba_mla_hgrn2_hmoe_stream'


def _rmsnorm(x, w):
    xf = x.astype(jnp.float32)
    y = xf * lax.rsqrt(jnp.mean(xf * xf, axis=-1, keepdims=True) + EPS)
    return (y * w.astype(jnp.float32)).astype(x.dtype)


def _rope(x, pos):
    half = ROPE_DIM // 2
    inv = ROPE_THETA ** (-jnp.arange(half, dtype=jnp.float32) / half)
    ang = pos.astype(jnp.float32)[:, None] * inv[None, :]
    cos = jnp.cos(ang)[None, :, None, :]
    sin = jnp.sin(ang)[None, :, None, :]
    xf = x.astype(jnp.float32)
    x1, x2 = xf[..., :half], xf[..., half:]
    return jnp.concatenate([x1 * cos - x2 * sin, x2 * cos + x1 * sin], axis=-1).astype(x.dtype)


def _chunk_ids(n):
    i = np.arange(n)
    return np.where(i < N_META, 0, 1 + (i - N_META) // CHUNK)


def _attend_prompt(q_lat, q_rope, ckv, krope):
    n = q_lat.shape[1]
    cid = _chunk_ids(n)
    outs = []
    for start in range(0, n, Q_BLOCK):
        stop = min(start + Q_BLOCK, n)
        kend = min(n, N_META + int(cid[stop - 1]) * CHUNK)
        ck, kr = ckv[:, :kend], krope[:, :kend]
        s = (jnp.einsum('bqhc,bkc->bhqk', q_lat[:, start:stop], ck)
             + jnp.einsum('bqhr,bkr->bhqk', q_rope[:, start:stop], kr)).astype(jnp.float32) * SM_SCALE
        mask = jnp.asarray(cid[start:stop, None] >= cid[None, :kend])
        p = jax.nn.softmax(jnp.where(mask[None, None], s, -jnp.inf), axis=-1).astype(ck.dtype)
        outs.append(jnp.einsum('bhqk,bkc->bqhc', p, ck))
    return jnp.concatenate(outs, axis=1)


def _attend_sample(cache_ckv, cache_krope, q_lat, q_rope, ckv, krope):
    ck = jnp.concatenate([cache_ckv.astype(ckv.dtype), ckv], axis=1)
    kr = jnp.concatenate([cache_krope.astype(krope.dtype), krope], axis=1)
    s = (jnp.einsum('bqhc,bkc->bhqk', q_lat, ck)
         + jnp.einsum('bqhr,bkr->bhqk', q_rope, kr)).astype(jnp.float32) * SM_SCALE
    p = jax.nn.softmax(s, axis=-1).astype(ck.dtype)
    return jnp.einsum('bhqk,bkc->bqhc', p, ck)


def _gated_scan(q, k, v, logf, s0):
    B, T, H, K = q.shape
    L = HG_BLOCK
    Tp = ((T + L - 1) // L) * L
    pad = ((0, 0), (0, Tp - T), (0, 0), (0, 0))
    q, k, v, logf = (jnp.pad(a, pad) for a in (q, k, v, logf))
    n = Tp // L
    blocks = lambda a: a.reshape(B, n, L, H, a.shape[-1]).swapaxes(0, 1)
    causal = jnp.tril(jnp.ones((L, L), dtype=bool))[None, :, :, None, None]

    def step(S, blk):
        qb, kb, vb, gb = blk
        b = jnp.cumsum(gb, axis=1)
        o_inter = jnp.einsum('bthk,bhkv->bthv', qb * jnp.exp(b), S)
        decay = jnp.exp(jnp.where(causal, b[:, :, None] - b[:, None, :], -jnp.inf))
        attn = jnp.einsum('bthk,bshk,btshk->bhts', qb, kb, decay)
        o_intra = jnp.einsum('bhts,bshv->bthv', attn, vb)
        bL = b[:, -1]
        S_new = jnp.exp(bL)[..., None] * S + jnp.einsum('bshk,bshv->bhkv', kb * jnp.exp(bL[:, None] - b), vb)
        return S_new, o_inter + o_intra

    S_fin, o = lax.scan(step, s0, (blocks(q), blocks(k), blocks(v), blocks(logf)))
    o = o.swapaxes(0, 1).reshape(B, Tp, H, v.shape[-1])[:, :T]
    return o, S_fin


def _hgrn2(hq, hf, hi, hg, lb, gnorm, s0):
    B, S, _ = hq.shape
    heads = lambda a: a.reshape(B, S, HG_HEADS, -1)
    q = jax.nn.silu(hq.astype(jnp.float32)) * (HG_KEY_DIM ** -0.5)
    f = lb + (1.0 - lb) * jax.nn.sigmoid(hf.astype(jnp.float32))
    k = 1.0 - f
    o, s_new = _gated_scan(heads(q), heads(k), heads(hi.astype(jnp.float32)), heads(jnp.log(f)),
                           s0.astype(jnp.float32))
    o = o * lax.rsqrt(jnp.mean(o * o, axis=-1, keepdims=True) + EPS) * gnorm.astype(jnp.float32)
    o = o * jax.nn.silu(heads(hg.astype(jnp.float32)))
    return o.reshape(B, S, HG_WIDTH).astype(hq.dtype), s_new.astype(hq.dtype)


def _hier_moe(x, w_rg, b_rg, w_re, b_re, w1, w3, w2):
    B, S, D = x.shape
    t = x.reshape(-1, D)
    p_group = jax.nn.softmax((t @ w_rg).astype(jnp.float32) + b_rg.astype(jnp.float32), axis=-1)
    g_top, g_idx = lax.top_k(p_group, 1)
    e_logits = ((t @ w_re).astype(jnp.float32) + b_re.astype(jnp.float32)).reshape(-1, N_GROUPS, EXPERTS_PER_GROUP)
    e_sel = jnp.take_along_axis(e_logits, g_idx[:, :, None], axis=1)[:, 0]
    e_top, e_idx = lax.top_k(jax.nn.softmax(e_sel, axis=-1), TOP_K)
    e_w = e_top / jnp.sum(e_top, axis=-1, keepdims=True)
    within = jnp.sum(jax.nn.one_hot(e_idx, EXPERTS_PER_GROUP, dtype=jnp.float32) * e_w[..., None], axis=1)
    gates = (jax.nn.one_hot(g_idx[:, 0], N_GROUPS, dtype=jnp.float32)[:, :, None]
             * within[:, None, :] * g_top[:, :, None]).reshape(-1, N_EXPERTS)
    y = jnp.zeros(t.shape, jnp.float32)
    for e in range(N_EXPERTS):
        he = jax.nn.silu(t @ w1[e]) * (t @ w3[e])
        y = y + gates[:, e:e + 1] * (he @ w2[e]).astype(jnp.float32)
    return y.reshape(B, S, D).astype(x.dtype)


def _layer(h, pos, attend, s0, lb, attn_norm, w_in, q_norm, kv_norm, w_uq, w_uk, w_uv, hg_gnorm, w_o,
           ffn_norm, w_rg, b_rg, w_re, b_re, w1, w3, w2):
    B, S, _ = h.shape
    hn = _rmsnorm(h, attn_norm)
    z = hn @ w_in
    cq, ckv_pre, kr_pre, hq, hf, hi, hg = jnp.split(z, np.cumsum(IN_SPLITS)[:-1].tolist(), axis=-1)
    q = jnp.einsum('bsc,chd->bshd', _rmsnorm(cq, q_norm), w_uq)
    q_rope = _rope(q[..., QK_NOPE_DIM:], pos)
    q_lat = jnp.einsum('bshn,chn->bshc', q[..., :QK_NOPE_DIM], w_uk)
    ckv = _rmsnorm(ckv_pre, kv_norm)
    krope = _rope(kr_pre[:, :, None, :], pos)[:, :, 0, :]
    o_lat = attend(q_lat, q_rope, ckv, krope)
    o_mla = jnp.einsum('bshc,chv->bshv', o_lat, w_uv).reshape(B, S, MLA_WIDTH)
    o_hg, s_new = _hgrn2(hq, hf, hi, hg, lb, hg_gnorm, s0)
    h = h + jnp.concatenate([o_mla, o_hg], axis=-1) @ w_o
    h = h + _hier_moe(_rmsnorm(h, ffn_norm), w_rg, b_rg, w_re, b_re, w1, w3, w2)
    return h, ckv, krope, s_new


def setup_inputs(seed: int = 0) -> dict:
    key = jax.random.key(seed)
    ks = jax.random.split(key, 26)
    f32 = jnp.float32
    nrm = lambda k, shape, scale: jax.random.normal(k, shape, f32) * scale
    gain = lambda k, shape: 1.0 + 0.01 * jax.random.normal(k, shape, f32)
    return {
        'x_prompt': nrm(ks[0], (BATCH, SEQ, D_MODEL), 1.0),
        'x_sample': nrm(ks[1], (DEC_BATCH, DEC_SEQ, D_MODEL), 1.0),
        'cache_ckv': nrm(ks[2], (DEPTH, DEC_BATCH, PAST_LEN, KV_LORA_RANK), 1.0),
        'cache_krope': nrm(ks[3], (DEPTH, DEC_BATCH, PAST_LEN, ROPE_DIM), 1.0),
        'state_hgrn': nrm(ks[4], (DEPTH, DEC_BATCH, HG_HEADS, HG_KEY_DIM, HG_VAL_DIM), 0.3),
        'meta_tokens': nrm(ks[5], (N_META, D_MODEL), 1.0),
        'attn_norm': gain(ks[6], (DEPTH, D_MODEL)),
        'w_in': nrm(ks[7], (DEPTH, D_MODEL, IN_WIDTH), D_MODEL ** -0.5),
        'q_norm': gain(ks[8], (DEPTH, Q_LORA_RANK)),
        'kv_norm': gain(ks[9], (DEPTH, KV_LORA_RANK)),
        'w_uq': nrm(ks[10], (DEPTH, Q_LORA_RANK, MLA_HEADS, QK_NOPE_DIM + ROPE_DIM), Q_LORA_RANK ** -0.5),
        'w_uk': nrm(ks[11], (DEPTH, KV_LORA_RANK, MLA_HEADS, QK_NOPE_DIM), KV_LORA_RANK ** -0.5),
        'w_uv': nrm(ks[12], (DEPTH, KV_LORA_RANK, MLA_HEADS, V_HEAD_DIM), KV_LORA_RANK ** -0.5),
        'hg_lb_logits': nrm(ks[13], (DEPTH + 1, HG_KEY_WIDTH), 0.1),
        'hg_gnorm': gain(ks[14], (DEPTH, HG_VAL_DIM)),
        'w_o': nrm(ks[15], (DEPTH, MIX_WIDTH, D_MODEL), MIX_WIDTH ** -0.5),
        'ffn_norm': gain(ks[16], (DEPTH, D_MODEL)),
        'w_router_group': nrm(ks[17], (DEPTH, D_MODEL, N_GROUPS), D_MODEL ** -0.5),
        'b_router_group': nrm(ks[18], (DEPTH, N_GROUPS), 0.01),
        'w_router_expert': nrm(ks[19], (DEPTH, D_MODEL, N_EXPERTS), D_MODEL ** -0.5),
        'b_router_expert': nrm(ks[20], (DEPTH, N_EXPERTS), 0.01),
        'w1': nrm(ks[21], (DEPTH, N_EXPERTS, D_MODEL, EXPERT_FF), D_MODEL ** -0.5),
        'w3': nrm(ks[22], (DEPTH, N_EXPERTS, D_MODEL, EXPERT_FF), D_MODEL ** -0.5),
        'w2': nrm(ks[23], (DEPTH, N_EXPERTS, EXPERT_FF, D_MODEL), EXPERT_FF ** -0.5),
        'final_norm': gain(ks[24], (D_MODEL,)),
    }


def reference(x_prompt, x_sample, cache_ckv, cache_krope, state_hgrn, meta_tokens, attn_norm, w_in, q_norm,
              kv_norm, w_uq, w_uk, w_uv, hg_lb_logits, hg_gnorm, w_o, ffn_norm, w_router_group, b_router_group,
              w_router_expert, b_router_expert, w1, w3, w2, final_norm):
    lb_all = jnp.cumsum(jax.nn.softmax(hg_lb_logits.astype(jnp.float32), axis=0), axis=0)
    b_p = x_prompt.shape[0]
    hp = jnp.concatenate([jnp.broadcast_to(meta_tokens.astype(x_prompt.dtype)[None], (b_p, N_META, D_MODEL)),
                          x_prompt], axis=1)
    hs = x_sample
    pos_p = jnp.arange(hp.shape[1])
    pos_s = PAST_LEN + jnp.arange(hs.shape[1])
    s0_p = jnp.zeros((b_p, HG_HEADS, HG_KEY_DIM, HG_VAL_DIM), jnp.float32)
    ckv_p, kr_p, st_p, ckv_s, kr_s, st_s = [], [], [], [], [], []
    for l in range(DEPTH):
        wts = (lb_all[l], attn_norm[l], w_in[l], q_norm[l], kv_norm[l], w_uq[l], w_uk[l], w_uv[l], hg_gnorm[l],
               w_o[l], ffn_norm[l], w_router_group[l], b_router_group[l], w_router_expert[l],
               b_router_expert[l], w1[l], w3[l], w2[l])
        hp, c1, r1, s1 = _layer(hp, pos_p, _attend_prompt, s0_p, *wts)
        hs, c2, r2, s2 = _layer(hs, pos_s, functools.partial(_attend_sample, cache_ckv[l], cache_krope[l]),
                                state_hgrn[l], *wts)
        ckv_p.append(c1); kr_p.append(r1); st_p.append(s1)
        ckv_s.append(c2); kr_s.append(r2); st_s.append(s2)
    y_prompt = _rmsnorm(hp[:, N_META:], final_norm)
    y_sample = _rmsnorm(hs, final_norm)
    return (y_prompt, y_sample, jnp.stack(ckv_p), jnp.stack(kr_p), jnp.stack(st_p),
            jnp.stack(ckv_s), jnp.stack(kr_s), jnp.stack(st_s))
```

```python
import functools

import numpy as np
import jax
import jax.numpy as jnp
from jax import lax
from jax.experimental import pallas as pl
from jax.experimental.pallas import tpu as pltpu

F32 = jnp.float32
BF16 = jnp.bfloat16

D_MODEL = 1024
PAST_LEN = 2048
CHUNK = 64
N_META = 16
EPS = 1e-6
MLA_HEADS = 8
Q_LORA_RANK = 256
KV_LORA_RANK = 128
QK_NOPE_DIM = 64
ROPE_DIM = 32
V_HEAD_DIM = 64
ROPE_THETA = 10000.0
MLA_WIDTH = MLA_HEADS * V_HEAD_DIM
SM_SCALE = (QK_NOPE_DIM + ROPE_DIM) ** -0.5
HG_HEADS = 4
HG_KEY_DIM = 128
HG_VAL_DIM = 128
HG_WIDTH = HG_HEADS * HG_VAL_DIM
HG_BLOCK = 16
N_GROUPS = 4
EXPERTS_PER_GROUP = 4
N_EXPERTS = N_GROUPS * EXPERTS_PER_GROUP
EXPERT_FF = 256

LANES = 128
KCAT = 2 * LANES
ONE_COL = LANES + ROPE_DIM
NEG = -1e30
VMEM_LIMIT = 56 * 1024 * 1024
GATE_LANE0 = N_GROUPS


def _rms(x, w):
    return x * lax.rsqrt(jnp.mean(x * x, axis=-1, keepdims=True) + EPS) * w


def _dot(a, b):
    return jnp.dot(a, b, preferred_element_type=F32)


def _dot_nt(a, b):
    return lax.dot_general(a, b, (((1,), (1,)), ((), ())), preferred_element_type=F32)


def _dot_tn(a, b):
    return lax.dot_general(a, b, (((0,), (0,)), ((), ())), preferred_element_type=F32)


def _split3(x):
    x1 = x.astype(BF16)
    r = x - x1.astype(F32)
    x2 = r.astype(BF16)
    x3 = (r - x2.astype(F32)).astype(BF16)
    return x1, x2, x3


def _params(n_axes):
    return pltpu.CompilerParams(dimension_semantics=("arbitrary",) * n_axes, vmem_limit_bytes=VMEM_LIMIT)


def _const_spec(shape):
    nd = len(shape)
    return pl.BlockSpec(shape, lambda *_: (0,) * nd)


def _inproj_body(x_ref, cosk_ref, sink_ref, cosq_ref, sinq_ref, an_ref, win_ref, qn_ref, kvn_ref, wuq_ref, w2_ref,
                 ckv_ref, krope_ref, kcat_ref, qcat_ref, hz_ref):
    x = x_ref[...]
    hn = _rms(x, an_ref[...]).astype(BF16)
    z = _dot(hn, win_ref[...])
    hz_ref[...] = z[:, 4 * LANES:]
    ckv = _rms(z[:, 2 * LANES:3 * LANES], kvn_ref[...])
    ckv_ref[...] = ckv
    kr = z[:, 3 * LANES:4 * LANES]
    rot = kr * cosk_ref[...] + pltpu.roll(kr, LANES - ROPE_DIM, 1) * sink_ref[...]
    krope_ref[...] = rot[:, :ROPE_DIM]
    lane = lax.broadcasted_iota(jnp.int32, rot.shape, 1)
    kcat_ref[...] = jnp.concatenate([ckv, jnp.where(lane == ROPE_DIM, 1.0, rot)], axis=-1).astype(BF16)
    cqn = _rms(z[:, :Q_LORA_RANK], qn_ref[...]).astype(BF16)
    qz = _dot(cqn, wuq_ref[...])
    cosq = cosq_ref[...]
    sinq = sinq_ref[...]
    for h in range(MLA_HEADS):
        blk = qz[:, h * LANES:(h + 1) * LANES]
        r = blk * cosq + pltpu.roll(blk, LANES - ROPE_DIM, 1) * sinq
        qcat_ref[h] = (_dot(r.astype(BF16), w2_ref[h]) * SM_SCALE).astype(BF16)


def _inproj(x2d, tabs, wts, tm):
    t = x2d.shape[0]
    period = tabs[0].shape[0] // tm
    grid = (t // tm,)
    row = lambda w: pl.BlockSpec((tm, w), lambda i: (i, 0))
    tab = pl.BlockSpec((tm, LANES), lambda i: (i % period, 0))
    an, win, qn, kvn, wuq, w2 = wts
    return pl.pallas_call(
        _inproj_body,
        grid=grid,
        in_specs=[row(D_MODEL), tab, tab, tab, tab, _const_spec(an.shape), _const_spec(win.shape),
                  _const_spec(qn.shape), _const_spec(kvn.shape), _const_spec(wuq.shape), _const_spec(w2.shape)],
        out_specs=[row(KV_LORA_RANK), row(ROPE_DIM), row(KCAT),
                   pl.BlockSpec((MLA_HEADS, tm, KCAT), lambda i: (0, i, 0)), row(4 * HG_WIDTH)],
        out_shape=[jax.ShapeDtypeStruct((t, KV_LORA_RANK), F32), jax.ShapeDtypeStruct((t, ROPE_DIM), F32),
                   jax.ShapeDtypeStruct((t, KCAT), BF16), jax.ShapeDtypeStruct((MLA_HEADS, t, KCAT), BF16),
                   jax.ShapeDtypeStruct((t, 4 * HG_WIDTH), F32)],
        compiler_params=_params(1),
        name="inproj",
    )(x2d, *tabs, an, win, qn, kvn, wuq, w2)


def _hgrn_body(hq_ref, hf_ref, hi_ref, hg_ref, lb_ref, gn_ref, s0_ref, tri_ref, blk_ref, ones_ref,
               o_ref, sout_ref, st_ref, b_scr, k_scr, v_scr, e_scr, r_scr, oi_scr, *, tt):
    step = pl.program_id(1)
    nb = tt // HG_BLOCK

    @pl.when(step == 0)
    def _():
        for h in range(HG_HEADS):
            st_ref[h] = s0_ref[0, h].T

    lb = lb_ref[...]
    hq = hq_ref[...]
    q = hq * jax.nn.sigmoid(hq) * (HG_KEY_DIM ** -0.5)
    f = lb + (1.0 - lb) * jax.nn.sigmoid(hf_ref[...])
    g = jnp.log(f)
    kk = 1.0 - f
    v = hi_ref[...]
    g1, g2, g3 = _split3(g)
    tri = tri_ref[...]
    blk = blk_ref[...]
    b = _dot(tri, g1) + _dot(tri, g2) + _dot(tri, g3)
    bl = _dot(blk, g1) + _dot(blk, g2) + _dot(blk, g3)
    b_scr[...] = b
    k_scr[...] = kk
    v_scr[...] = v

    def rows_of(ref, s):
        return jnp.concatenate(
            [jnp.broadcast_to(ref[pl.ds(n * HG_BLOCK + s, 1), :], (HG_BLOCK, HG_WIDTH)) for n in range(nb)], axis=0)

    tpos = lax.broadcasted_iota(jnp.int32, (tt, HG_WIDTH), 0) % HG_BLOCK
    for s in range(HG_BLOCK):
        d = jnp.exp(jnp.where(tpos >= s, b - rows_of(b_scr, s), NEG))
        e_scr[s * tt:(s + 1) * tt, :] = (q * rows_of(k_scr, s) * d).astype(BF16)
    ones2 = ones_ref[...]
    for c in range(HG_HEADS // 2):
        cols = slice(2 * c * LANES, 2 * (c + 1) * LANES)
        r_scr[:, cols] = _dot(e_scr[:, cols], ones2)
    o_intra = jnp.zeros((tt, HG_WIDTH), F32)
    for s in range(HG_BLOCK):
        o_intra = o_intra + r_scr[s * tt:(s + 1) * tt, :] * rows_of(v_scr, s)

    qp = (q * jnp.exp(b)).astype(BF16)
    kp = (kk * jnp.exp(bl - b)).astype(BF16)
    vb = v.astype(BF16)
    dl = jnp.exp(bl)
    for n in range(nb):
        rows = slice(n * HG_BLOCK, (n + 1) * HG_BLOCK)
        for h in range(HG_HEADS):
            cols = slice(h * LANES, (h + 1) * LANES)
            st = st_ref[h]
            oi_scr[rows, cols] = _dot_nt(qp[rows, cols], st.astype(BF16))
            st_ref[h] = dl[n * HG_BLOCK:n * HG_BLOCK + 1, cols] * st + _dot_tn(vb[rows, cols], kp[rows, cols])

    o = o_intra + oi_scr[...]
    hg = hg_ref[...]
    gn = gn_ref[...]
    for h in range(HG_HEADS):
        cols = slice(h * LANES, (h + 1) * LANES)
        oh = _rms(o[:, cols], gn)
        gh = hg[:, cols]
        o_ref[:, cols] = (oh * (gh * jax.nn.sigmoid(gh))).astype(BF16)

    @pl.when(step == pl.num_programs(1) - 1)
    def _():
        for h in range(HG_HEADS):
            sout_ref[0, h] = st_ref[h].T


def _hgrn(hz, lb, gn, s0, batch, tt):
    t_all = hz.shape[0]
    steps = t_all // batch // tt
    nb = tt // HG_BLOCK
    idx = np.arange(tt)
    same = (idx[:, None] // HG_BLOCK) == (idx[None, :] // HG_BLOCK)
    tri = jnp.asarray(same & (idx[None, :] <= idx[:, None]), BF16)
    blk = jnp.asarray(same, BF16)
    li = np.arange(2 * LANES)
    ones2 = jnp.asarray((li[:, None] // LANES) == (li[None, :] // LANES), BF16)
    col = lambda c: pl.BlockSpec((tt, HG_WIDTH), lambda b, s: (b * steps + s, c))
    shared_s0 = s0.shape[0] == 1
    state = lambda shared: pl.BlockSpec((1, HG_HEADS, HG_KEY_DIM, HG_VAL_DIM),
                                        (lambda b, s: (0, 0, 0, 0)) if shared else (lambda b, s: (b, 0, 0, 0)))
    return pl.pallas_call(
        functools.partial(_hgrn_body, tt=tt),
        grid=(batch, steps),
        in_specs=[col(0), col(1), col(2), col(3), _const_spec(lb.shape), _const_spec(gn.shape), state(shared_s0),
                  _const_spec(tri.shape), _const_spec(blk.shape), _const_spec(ones2.shape)],
        out_specs=[pl.BlockSpec((tt, HG_WIDTH), lambda b, s: (b * steps + s, 0)), state(False)],
        out_shape=[jax.ShapeDtypeStruct((t_all, HG_WIDTH), BF16),
                   jax.ShapeDtypeStruct((batch, HG_HEADS, HG_KEY_DIM, HG_VAL_DIM), F32)],
        scratch_shapes=[pltpu.VMEM((HG_HEADS, HG_VAL_DIM, HG_KEY_DIM), F32),
                        pltpu.VMEM((tt, HG_WIDTH), F32), pltpu.VMEM((tt, HG_WIDTH), F32),
                        pltpu.VMEM((tt, HG_WIDTH), F32),
                        pltpu.VMEM((HG_BLOCK * tt, HG_WIDTH), BF16), pltpu.VMEM((HG_BLOCK * tt, HG_WIDTH), F32),
                        pltpu.VMEM((tt, HG_WIDTH), F32)],
        compiler_params=_params(2),
        name="hgrn",
    )(hz, hz, hz, hz, lb, gn, s0, tri, blk, ones2)


def _softmax_step(q, kt, mask, m_scr, acc_scr):
    s = _dot_nt(q, kt)
    if mask is not None:
        s = jnp.where(mask, s, NEG)
    m_prev = m_scr[...]
    m_new = jnp.maximum(m_prev, jnp.max(s, axis=1, keepdims=True))
    alpha = jnp.exp(m_prev - m_new)
    p = jnp.exp(s - pltpu.repeat(m_new, s.shape[1] // LANES, axis=1))
    acc_scr[...] = pltpu.repeat(alpha, KCAT // LANES, axis=1) * acc_scr[...] + _dot(p.astype(BF16), kt)
    m_scr[...] = m_new


def _attn_finish(acc, o_ref, tq):
    o = acc[:, :KV_LORA_RANK] / acc[:, ONE_COL:ONE_COL + 1]
    for h in range(MLA_HEADS):
        o_ref[:, h * LANES:(h + 1) * LANES] = o[h * tq:(h + 1) * tq].astype(BF16)


def _attn_prompt_body(q_ref, kmeta_ref, k_ref, o_ref, m_scr, acc_scr, *, tq, tk):
    i = pl.program_id(1)
    rows = MLA_HEADS * tq
    q = q_ref[...].reshape(rows, KCAT)
    m_scr[...] = jnp.full(m_scr.shape, NEG, F32)
    acc_scr[...] = jnp.zeros(acc_scr.shape, F32)
    lane = lax.broadcasted_iota(jnp.int32, (rows, LANES), 1)
    _softmax_step(q, kmeta_ref[...], lane < N_META, m_scr, acc_scr)
    n_full = (i * tq) // tk

    def body(j, carry):
        _softmax_step(q, k_ref[pl.ds(pl.multiple_of(j * tk, tk), tk), :], None, m_scr, acc_scr)
        return carry

    lax.fori_loop(0, n_full, body, 0)
    qchunk = (i * tq + lax.broadcasted_iota(jnp.int32, (rows, tk), 0) % tq) // CHUNK
    kchunk = (n_full * tk + lax.broadcasted_iota(jnp.int32, (rows, tk), 1)) // CHUNK
    _softmax_step(q, k_ref[pl.ds(pl.multiple_of(n_full * tk, tk), tk), :], kchunk <= qchunk, m_scr, acc_scr)
    _attn_finish(acc_scr[...], o_ref, tq)


def _attn_prompt(qcat, kmeta, kcat, batch, tq, tk):
    t_all = kcat.shape[0]
    seq = t_all // batch
    nq = seq // tq
    rows = MLA_HEADS * tq
    return pl.pallas_call(
        functools.partial(_attn_prompt_body, tq=tq, tk=tk),
        grid=(batch, nq),
        in_specs=[pl.BlockSpec((MLA_HEADS, tq, KCAT), lambda b, i: (0, b * nq + i, 0)),
                  _const_spec(kmeta.shape),
                  pl.BlockSpec((seq, KCAT), lambda b, i: (b, 0))],
        out_specs=pl.BlockSpec((tq, MLA_HEADS * LANES), lambda b, i: (b * nq + i, 0)),
        out_shape=jax.ShapeDtypeStruct((t_all, MLA_HEADS * LANES), BF16),
        scratch_shapes=[pltpu.VMEM((rows, LANES), F32), pltpu.VMEM((rows, KCAT), F32)],
        compiler_params=_params(2),
        name="attn_prompt",
    )(qcat, kmeta, kcat)


def _attn_sample_body(q_ref, kc_ref, kn_ref, o_ref, *, tq):
    rows = MLA_HEADS * tq
    q = q_ref[...].reshape(rows, KCAT)
    kc = kc_ref[0]
    kn = kn_ref[...]
    s_c = _dot_nt(q, kc)
    s_n = _dot_nt(q, kn)
    m = jnp.maximum(jnp.max(s_c, axis=1, keepdims=True), jnp.max(s_n, axis=1, keepdims=True))
    acc = _dot(jnp.exp(s_c - m).astype(BF16), kc) + _dot(jnp.exp(s_n - m).astype(BF16), kn)
    _attn_finish(acc, o_ref, tq)


def _attn_sample(qcat, kcache, kcat, tq):
    batch = kcache.shape[0]
    t_all = kcat.shape[0]
    return pl.pallas_call(
        functools.partial(_attn_sample_body, tq=tq),
        grid=(batch,),
        in_specs=[pl.BlockSpec((MLA_HEADS, tq, KCAT), lambda b: (0, b, 0)),
                  pl.BlockSpec((1,) + kcache.shape[1:], lambda b: (b, 0, 0)),
                  pl.BlockSpec((tq, KCAT), lambda b: (b, 0))],
        out_specs=pl.BlockSpec((tq, MLA_HEADS * LANES), lambda b: (b, 0)),
        out_shape=jax.ShapeDtypeStruct((t_all, MLA_HEADS * LANES), BF16),
        compiler_params=_params(1),
        name="attn_sample",
    )(qcat, kcache, kcat)


def _outproj_body(x_ref, olat_ref, ohg_ref, wuv_ref, wo_ref, fn_ref, wrh_ref, wrl_ref, br_ref,
                  h1_ref, hn_ref, gate_ref):
    o_mla = _dot(olat_ref[...], wuv_ref[...]).astype(BF16)
    mix = _dot(o_mla, wo_ref[:MLA_WIDTH, :]) + _dot(ohg_ref[...], wo_ref[MLA_WIDTH:, :])
    h1 = x_ref[...] + mix
    h1_ref[...] = h1
    hn = _rms(h1, fn_ref[...])
    hn_hi = hn.astype(BF16)
    hn_ref[...] = hn_hi
    hn_lo = (hn - hn_hi.astype(F32)).astype(BF16)
    wrh = wrh_ref[...]
    logits = _dot(hn_hi, wrh) + _dot(hn_hi, wrl_ref[...]) + _dot(hn_lo, wrh) + br_ref[...]
    lane = lax.broadcasted_iota(jnp.int32, logits.shape, 1)
    gl = jnp.where(lane < N_GROUPS, logits, NEG)
    gmax = jnp.max(gl, axis=1, keepdims=True)
    g_top = 1.0 / jnp.sum(jnp.exp(gl - gmax), axis=1, keepdims=True)
    g_idx = jnp.min(jnp.where(gl == gmax, lane, LANES), axis=1, keepdims=True)
    lo = GATE_LANE0 + EXPERTS_PER_GROUP * g_idx
    el = jnp.where((lane >= lo) & (lane < lo + EXPERTS_PER_GROUP), logits, NEG)
    e1 = jnp.max(el, axis=1, keepdims=True)
    i1 = jnp.min(jnp.where(el == e1, lane, LANES), axis=1, keepdims=True)
    el2 = jnp.where(lane == i1, NEG, el)
    e2 = jnp.max(el2, axis=1, keepdims=True)
    i2 = jnp.min(jnp.where(el2 == e2, lane, LANES), axis=1, keepdims=True)
    esum = jnp.sum(jnp.exp(el - e1), axis=1, keepdims=True)
    p1 = 1.0 / esum
    p2 = jnp.exp(e2 - e1) / esum
    w1 = p1 / (p1 + p2) * g_top
    w2 = p2 / (p1 + p2) * g_top
    gate_ref[...] = jnp.where(lane == i1, w1, jnp.where(lane == i2, w2, 0.0))


def _outproj(x2d, olat, ohg, wts, tm):
    t = x2d.shape[0]
    row = lambda w: pl.BlockSpec((tm, w), lambda i: (i, 0))
    return pl.pallas_call(
        _outproj_body,
        grid=(t // tm,),
        in_specs=[row(D_MODEL), row(MLA_HEADS * LANES), row(HG_WIDTH)] + [_const_spec(w.shape) for w in wts],
        out_specs=[row(D_MODEL), row(D_MODEL), row(LANES)],
        out_shape=[jax.ShapeDtypeStruct((t, D_MODEL), F32), jax.ShapeDtypeStruct((t, D_MODEL), BF16),
                   jax.ShapeDtypeStruct((t, LANES), F32)],
        compiler_params=_params(1),
        name="outproj",
    )(x2d, olat, ohg, *wts)


def _moe_body(hn_ref, h1_ref, gate_ref, w1_ref, w3_ref, w2_ref, fin_ref, y_ref):
    t = hn_ref[...]
    gates = gate_ref[...]
    acc = jnp.zeros(h1_ref.shape, F32)
    for grp in range(N_GROUPS):
        a = _dot(t, w1_ref[grp])
        c = _dot(t, w3_ref[grp])
        he = a * jax.nn.sigmoid(a) * c
        parts = []
        for j in range(EXPERTS_PER_GROUP):
            lane = GATE_LANE0 + grp * EXPERTS_PER_GROUP + j
            parts.append((he[:, j * EXPERT_FF:(j + 1) * EXPERT_FF] * gates[:, lane:lane + 1]).astype(BF16))
        acc = acc + _dot(jnp.concatenate(parts, axis=-1), w2_ref[grp])
    y_ref[...] = _rms(h1_ref[...] + acc, fin_ref[...])


def _moe(hn, h1, gates, w1g, w3g, w2g, fin, tm):
    t = hn.shape[0]
    row = lambda w: pl.BlockSpec((tm, w), lambda i: (i, 0))
    resident = lambda a: pl.BlockSpec(a.shape, lambda i: (0,) * a.ndim, pipeline_mode=pl.Buffered(1))
    return pl.pallas_call(
        _moe_body,
        grid=(t // tm,),
        in_specs=[row(D_MODEL), row(D_MODEL), row(LANES), resident(w1g), resident(w3g), resident(w2g),
                  _const_spec(fin.shape)],
        out_specs=row(D_MODEL),
        out_shape=jax.ShapeDtypeStruct((t, D_MODEL), F32),
        compiler_params=_params(1),
        name="moe",
    )(hn, h1, gates, w1g, w3g, w2g, fin)


def _rope_tables(pos):
    half = ROPE_DIM // 2
    inv = ROPE_THETA ** (-jnp.arange(half, dtype=F32) / half)
    ang = pos.astype(F32)[:, None] * inv[None, :]
    cos, sin = jnp.cos(ang), jnp.sin(ang)
    n = pos.shape[0]
    z = lambda w: jnp.zeros((n, w), F32)
    cosk = jnp.concatenate([cos, cos, z(LANES - ROPE_DIM)], axis=1)
    sink = jnp.concatenate([-sin, sin, z(LANES - ROPE_DIM)], axis=1)
    cosq = jnp.concatenate([jnp.ones((n, QK_NOPE_DIM), F32), cos, cos, z(ROPE_DIM)], axis=1)
    sinq = jnp.concatenate([z(QK_NOPE_DIM), -sin, sin, z(ROPE_DIM)], axis=1)
    return cosk, sink, cosq, sinq


def kernel(x_prompt, x_sample, cache_ckv, cache_krope, state_hgrn, meta_tokens, attn_norm, w_in, q_norm, kv_norm, w_uq, w_uk, w_uv, hg_lb_logits, hg_gnorm, w_o, ffn_norm, w_router_group, b_router_group, w_router_expert, b_router_expert, w1, w3, w2, final_norm):
    bp, seq, _ = x_prompt.shape
    bs, dseq, _ = x_sample.shape
    half = ROPE_DIM // 2
    swap = np.concatenate([np.arange(half, ROPE_DIM), np.arange(half)])

    lb = jax.nn.softmax(hg_lb_logits.astype(F32), axis=0)[0][None, :]
    wi = w_in[0]
    o_kv, o_kr, o_hg = Q_LORA_RANK, Q_LORA_RANK + KV_LORA_RANK, Q_LORA_RANK + KV_LORA_RANK + ROPE_DIM
    w_kr = wi[:, o_kr:o_hg]
    win_ext = jnp.concatenate([wi[:, :o_kr], w_kr, w_kr[:, swap], jnp.zeros((D_MODEL, LANES - 2 * ROPE_DIM), F32),
                               wi[:, o_hg:]], axis=1).astype(BF16)
    wq = w_uq[0]
    wq_rope = wq[:, :, QK_NOPE_DIM:]
    wuq_ext = jnp.concatenate([wq, wq_rope[:, :, swap]], axis=2).reshape(Q_LORA_RANK, MLA_HEADS * LANES).astype(BF16)
    eye = jnp.eye(ROPE_DIM, dtype=F32)
    w2q = jnp.zeros((MLA_HEADS, LANES, KCAT), F32)
    w2q = w2q.at[:, :QK_NOPE_DIM, :KV_LORA_RANK].set(jnp.transpose(w_uk[0], (1, 2, 0)))
    w2q = w2q.at[:, QK_NOPE_DIM:QK_NOPE_DIM + ROPE_DIM, KV_LORA_RANK:KV_LORA_RANK + ROPE_DIM].set(eye[None])
    w2q = w2q.astype(BF16)
    wuv_bd = jnp.zeros((MLA_HEADS, KV_LORA_RANK, MLA_HEADS, V_HEAD_DIM), F32)
    for h in range(MLA_HEADS):
        wuv_bd = wuv_bd.at[h, :, h, :].set(w_uv[0][:, h, :])
    wuv_bd = wuv_bd.reshape(MLA_HEADS * KV_LORA_RANK, MLA_WIDTH).astype(BF16)
    wo = w_o[0].astype(BF16)
    wr = jnp.concatenate([w_router_group[0], w_router_expert[0],
                          jnp.zeros((D_MODEL, LANES - N_GROUPS - N_EXPERTS), F32)], axis=1)
    wr_hi = wr.astype(BF16)
    wr_lo = (wr - wr_hi.astype(F32)).astype(BF16)
    br = jnp.concatenate([b_router_group[0], b_router_expert[0],
                          jnp.zeros((LANES - N_GROUPS - N_EXPERTS,), F32)])[None, :].astype(F32)
    grp_cols = lambda w: jnp.transpose(w.reshape(N_GROUPS, EXPERTS_PER_GROUP, D_MODEL, EXPERT_FF),
                                       (0, 2, 1, 3)).reshape(N_GROUPS, D_MODEL, EXPERTS_PER_GROUP * EXPERT_FF)
    w1g = grp_cols(w1[0].astype(BF16))
    w3g = grp_cols(w3[0].astype(BF16))
    w2g = w2[0].astype(BF16).reshape(N_GROUPS, EXPERTS_PER_GROUP * EXPERT_FF, D_MODEL)
    in_wts = (attn_norm[0][None, :], win_ext, q_norm[0][None, :], kv_norm[0][None, :], wuq_ext, w2q)
    out_wts = (wuv_bd, wo, ffn_norm[0][None, :], wr_hi, wr_lo, br)
    gn = hg_gnorm[0][None, :]
    fin = final_norm[None, :]

    tm_s = 256
    tab_meta = _rope_tables(jnp.arange(N_META))
    tab_p = _rope_tables(N_META + jnp.arange(seq))
    tab_s = tuple(jnp.tile(a, (tm_s // dseq, 1)) for a in _rope_tables(PAST_LEN + jnp.arange(dseq)))
    ckv_m, kr_m, kcat_m, _, hz_m = _inproj(meta_tokens, tab_meta, in_wts, N_META)
    xp = x_prompt.reshape(bp * seq, D_MODEL)
    xs = x_sample.reshape(bs * dseq, D_MODEL)
    ckv_p, kr_p, kcat_p, qcat_p, hz_p = _inproj(xp, tab_p, in_wts, 512)
    ckv_s, kr_s, kcat_s, qcat_s, hz_s = _inproj(xs, tab_s, in_wts, tm_s)

    zero_state = jnp.zeros((1, HG_HEADS, HG_KEY_DIM, HG_VAL_DIM), F32)
    _, st_m = _hgrn(hz_m, lb, gn, zero_state, 1, N_META)
    ohg_p, st_p = _hgrn(hz_p, lb, gn, st_m, bp, 128)
    ohg_s, st_s = _hgrn(hz_s, lb, gn, state_hgrn[0].astype(F32), bs, dseq)

    kmeta = jnp.concatenate([kcat_m, jnp.zeros((LANES - N_META, KCAT), BF16)], axis=0)
    olat_p = _attn_prompt(qcat_p, kmeta, kcat_p, bp, 128, 512)
    n_c = cache_ckv.shape[2]
    kcache = jnp.concatenate([cache_ckv[0], cache_krope[0], jnp.ones((bs, n_c, 1), F32),
                              jnp.zeros((bs, n_c, KCAT - ONE_COL - 1), F32)], axis=-1).astype(BF16)
    olat_s = _attn_sample(qcat_s, kcache, kcat_s, dseq)

    h1_p, hn_p, gate_p = _outproj(xp, olat_p, ohg_p, out_wts, 256)
    h1_s, hn_s, gate_s = _outproj(xs, olat_s, ohg_s, out_wts, 256)
    y_p = _moe(hn_p, h1_p, gate_p, w1g, w3g, w2g, fin, 256)
    y_s = _moe(hn_s, h1_s, gate_s, w1g, w3g, w2g, fin, 256)

    with_meta = lambda m, a, w: jnp.concatenate(
        [jnp.broadcast_to(m[None], (bp, N_META, w)), a.reshape(bp, seq, w)], axis=1)[None]
    return (y_p.reshape(bp, seq, D_MODEL), y_s.reshape(bs, dseq, D_MODEL),
            with_meta(ckv_m, ckv_p, KV_LORA_RANK), with_meta(kr_m, kr_p, ROPE_DIM), st_p[None],
            ckv_s.reshape(1, bs, dseq, KV_LORA_RANK), kr_s.reshape(1, bs, dseq, ROPE_DIM), st_s[None])
```

```python
import functools

import numpy as np
import jax
import jax.numpy as jnp
from jax import lax
from jax.experimental import pallas as pl
from jax.experimental.pallas import tpu as pltpu

F32 = jnp.float32
BF16 = jnp.bfloat16

D_MODEL = 1024
PAST_LEN = 2048
CHUNK = 64
N_META = 16
EPS = 1e-6
MLA_HEADS = 8
Q_LORA_RANK = 256
KV_LORA_RANK = 128
QK_NOPE_DIM = 64
ROPE_DIM = 32
V_HEAD_DIM = 64
ROPE_THETA = 10000.0
MLA_WIDTH = MLA_HEADS * V_HEAD_DIM
SM_SCALE = (QK_NOPE_DIM + ROPE_DIM) ** -0.5
LOG2E = 1.4426950408889634
Q_SCALE = SM_SCALE * LOG2E
HG_HEADS = 4
HG_KEY_DIM = 128
HG_VAL_DIM = 128
HG_WIDTH = HG_HEADS * HG_VAL_DIM
HG_BLOCK = 16
N_GROUPS = 4
EXPERTS_PER_GROUP = 4
N_EXPERTS = N_GROUPS * EXPERTS_PER_GROUP
EXPERT_FF = 256

LANES = 128
KCAT = 2 * LANES
ONE_COL = LANES + ROPE_DIM
NEG = -1e30
VMEM_LIMIT = 56 * 1024 * 1024
GATE_LANE0 = N_GROUPS


def _rms(x, w):
    return x * lax.rsqrt(jnp.mean(x * x, axis=-1, keepdims=True) + EPS) * w


def _dot(a, b):
    return jnp.dot(a, b, preferred_element_type=F32)


def _dot_nt(a, b):
    return lax.dot_general(a, b, (((1,), (1,)), ((), ())), preferred_element_type=F32)


def _dot_tn(a, b):
    return lax.dot_general(a, b, (((0,), (0,)), ((), ())), preferred_element_type=F32)


def _split3(x):
    x1 = x.astype(BF16)
    r = x - x1.astype(F32)
    x2 = r.astype(BF16)
    x3 = (r - x2.astype(F32)).astype(BF16)
    return x1, x2, x3


def _params(n_axes):
    return pltpu.CompilerParams(dimension_semantics=("arbitrary",) * n_axes, vmem_limit_bytes=VMEM_LIMIT)


def _const_spec(shape):
    nd = len(shape)
    return pl.BlockSpec(shape, lambda *_: (0,) * nd)


def _inproj_body(x_ref, cosk_ref, sink_ref, cosq_ref, sinq_ref, an_ref, win_ref, qn_ref, kvn_ref, wuq_ref, w2_ref,
                 ckv_ref, krope_ref, kcat_ref, qcat_ref, hz_ref):
    x = x_ref[...]
    hn = _rms(x, an_ref[...]).astype(BF16)
    z = _dot(hn, win_ref[...])
    hz_ref[...] = z[:, 4 * LANES:]
    ckv = _rms(z[:, 2 * LANES:3 * LANES], kvn_ref[...])
    ckv_ref[...] = ckv
    kr = z[:, 3 * LANES:4 * LANES]
    rot = kr * cosk_ref[...] + pltpu.roll(kr, LANES - ROPE_DIM, 1) * sink_ref[...]
    krope_ref[...] = rot[:, :ROPE_DIM]
    lane = lax.broadcasted_iota(jnp.int32, rot.shape, 1)
    kcat_ref[...] = jnp.concatenate([ckv, jnp.where(lane == ROPE_DIM, 1.0, rot)], axis=-1).astype(BF16)
    cqn = _rms(z[:, :Q_LORA_RANK], qn_ref[...]).astype(BF16)
    qz = _dot(cqn, wuq_ref[...])
    cosq = cosq_ref[...]
    sinq = sinq_ref[...]
    for h in range(MLA_HEADS):
        blk = qz[:, h * LANES:(h + 1) * LANES]
        r = blk * cosq + pltpu.roll(blk, LANES - ROPE_DIM, 1) * sinq
        qcat_ref[h] = (_dot(r.astype(BF16), w2_ref[h]) * Q_SCALE).astype(BF16)


def _inproj(x2d, tabs, wts, tm):
    t = x2d.shape[0]
    period = tabs[0].shape[0] // tm
    grid = (t // tm,)
    row = lambda w: pl.BlockSpec((tm, w), lambda i: (i, 0))
    tab = pl.BlockSpec((tm, LANES), lambda i: (i % period, 0))
    an, win, qn, kvn, wuq, w2 = wts
    return pl.pallas_call(
        _inproj_body,
        grid=grid,
        in_specs=[row(D_MODEL), tab, tab, tab, tab, _const_spec(an.shape), _const_spec(win.shape),
                  _const_spec(qn.shape), _const_spec(kvn.shape), _const_spec(wuq.shape), _const_spec(w2.shape)],
        out_specs=[row(KV_LORA_RANK), row(ROPE_DIM), row(KCAT),
                   pl.BlockSpec((MLA_HEADS, tm, KCAT), lambda i: (0, i, 0)), row(4 * HG_WIDTH)],
        out_shape=[jax.ShapeDtypeStruct((t, KV_LORA_RANK), F32), jax.ShapeDtypeStruct((t, ROPE_DIM), F32),
                   jax.ShapeDtypeStruct((t, KCAT), BF16), jax.ShapeDtypeStruct((MLA_HEADS, t, KCAT), BF16),
                   jax.ShapeDtypeStruct((t, 4 * HG_WIDTH), F32)],
        compiler_params=_params(1),
        name="inproj",
    )(x2d, *tabs, an, win, qn, kvn, wuq, w2)


def _hgrn_body(hq_ref, hf_ref, hi_ref, hg_ref, lb_ref, gn_ref, s0_ref, tri_ref, blk_ref, ones_ref,
               o_ref, sout_ref, st_ref, b_scr, k_scr, v_scr, e_scr, r_scr, oi_scr, *, tt):
    step = pl.program_id(1)
    nb = tt // HG_BLOCK

    @pl.when(step == 0)
    def _():
        for h in range(HG_HEADS):
            st_ref[h] = s0_ref[0, h].T

    lb = lb_ref[...]
    hq = hq_ref[...]
    q = hq * jax.nn.sigmoid(hq) * (HG_KEY_DIM ** -0.5)
    f = lb + (1.0 - lb) * jax.nn.sigmoid(hf_ref[...])
    g = jnp.log(f) * LOG2E
    kk = 1.0 - f
    v = hi_ref[...]
    g1, g2, g3 = _split3(g)
    tri = tri_ref[...]
    blk = blk_ref[...]
    b = _dot(tri, g1) + _dot(tri, g2) + _dot(tri, g3)
    bl = _dot(blk, g1) + _dot(blk, g2) + _dot(blk, g3)
    b_scr[...] = b
    k_scr[...] = kk
    v_scr[...] = v

    half = HG_BLOCK // 2
    th = nb * half

    def halves(x):
        x3 = x.reshape(nb, HG_BLOCK, HG_WIDTH)
        return x3[:, :half].reshape(th, HG_WIDTH), x3[:, half:].reshape(th, HG_WIDTH)

    def rows_of(ref, s):
        return jnp.concatenate(
            [jnp.broadcast_to(ref[pl.ds(n * HG_BLOCK + s, 1), :], (half, HG_WIDTH)) for n in range(nb)], axis=0)

    q_lo, q_hi = halves(q)
    b_lo, b_hi = halves(b)
    tpos = lax.broadcasted_iota(jnp.int32, (th, HG_WIDTH), 0) % half
    for s in range(HG_BLOCK):
        bs = rows_of(b_scr, s)
        ks = rows_of(k_scr, s)
        if s < half:
            d_lo = jnp.exp2(jnp.where(tpos >= s, b_lo - bs, NEG))
            e_scr[s * th:(s + 1) * th, :] = (q_lo * ks * d_lo).astype(BF16)
            d_hi = jnp.exp2(b_hi - bs)
        else:
            d_hi = jnp.exp2(jnp.where(tpos >= s - half, b_hi - bs, NEG))
        e_scr[(half + s) * th:(half + s + 1) * th, :] = (q_hi * ks * d_hi).astype(BF16)
    ones2 = ones_ref[...]
    for c in range(HG_HEADS // 2):
        cols = slice(2 * c * LANES, 2 * (c + 1) * LANES)
        r_scr[:, cols] = _dot(e_scr[:, cols], ones2)
    o_lo = jnp.zeros((th, HG_WIDTH), F32)
    o_hi = jnp.zeros((th, HG_WIDTH), F32)
    for s in range(HG_BLOCK):
        vs = rows_of(v_scr, s)
        if s < half:
            o_lo = o_lo + r_scr[s * th:(s + 1) * th, :] * vs
        o_hi = o_hi + r_scr[(half + s) * th:(half + s + 1) * th, :] * vs
    o_intra = jnp.concatenate([o_lo.reshape(nb, half, HG_WIDTH), o_hi.reshape(nb, half, HG_WIDTH)],
                              axis=1).reshape(tt, HG_WIDTH)

    qp = (q * jnp.exp2(b)).astype(BF16)
    kp = (kk * jnp.exp2(bl - b)).astype(BF16)
    vb = v.astype(BF16)
    dl = jnp.exp2(bl)
    for n in range(nb):
        rows = slice(n * HG_BLOCK, (n + 1) * HG_BLOCK)
        for h in range(HG_HEADS):
            cols = slice(h * LANES, (h + 1) * LANES)
            st = st_ref[h]
            oi_scr[rows, cols] = _dot_nt(qp[rows, cols], st.astype(BF16))
            st_ref[h] = dl[n * HG_BLOCK:n * HG_BLOCK + 1, cols] * st + _dot_tn(vb[rows, cols], kp[rows, cols])

    o = o_intra + oi_scr[...]
    hg = hg_ref[...]
    gn = gn_ref[...]
    for h in range(HG_HEADS):
        cols = slice(h * LANES, (h + 1) * LANES)
        oh = _rms(o[:, cols], gn)
        gh = hg[:, cols]
        o_ref[:, cols] = (oh * (gh * jax.nn.sigmoid(gh))).astype(BF16)

    @pl.when(step == pl.num_programs(1) - 1)
    def _():
        for h in range(HG_HEADS):
            sout_ref[0, h] = st_ref[h].T


def _hgrn(hz, lb, gn, s0, batch, tt):
    t_all = hz.shape[0]
    steps = t_all // batch // tt
    nb = tt // HG_BLOCK
    idx = np.arange(tt)
    same = (idx[:, None] // HG_BLOCK) == (idx[None, :] // HG_BLOCK)
    tri = jnp.asarray(same & (idx[None, :] <= idx[:, None]), BF16)
    blk = jnp.asarray(same, BF16)
    li = np.arange(2 * LANES)
    ones2 = jnp.asarray((li[:, None] // LANES) == (li[None, :] // LANES), BF16)
    col = lambda c: pl.BlockSpec((tt, HG_WIDTH), lambda b, s: (b * steps + s, c))
    shared_s0 = s0.shape[0] == 1
    state = lambda shared: pl.BlockSpec((1, HG_HEADS, HG_KEY_DIM, HG_VAL_DIM),
                                        (lambda b, s: (0, 0, 0, 0)) if shared else (lambda b, s: (b, 0, 0, 0)))
    return pl.pallas_call(
        functools.partial(_hgrn_body, tt=tt),
        grid=(batch, steps),
        in_specs=[col(0), col(1), col(2), col(3), _const_spec(lb.shape), _const_spec(gn.shape), state(shared_s0),
                  _const_spec(tri.shape), _const_spec(blk.shape), _const_spec(ones2.shape)],
        out_specs=[pl.BlockSpec((tt, HG_WIDTH), lambda b, s: (b * steps + s, 0)), state(False)],
        out_shape=[jax.ShapeDtypeStruct((t_all, HG_WIDTH), BF16),
                   jax.ShapeDtypeStruct((batch, HG_HEADS, HG_KEY_DIM, HG_VAL_DIM), F32)],
        scratch_shapes=[pltpu.VMEM((HG_HEADS, HG_VAL_DIM, HG_KEY_DIM), F32),
                        pltpu.VMEM((tt, HG_WIDTH), F32), pltpu.VMEM((tt, HG_WIDTH), F32),
                        pltpu.VMEM((tt, HG_WIDTH), F32),
                        pltpu.VMEM((HG_BLOCK * 3 // 4 * tt, HG_WIDTH), BF16),
                        pltpu.VMEM((HG_BLOCK * 3 // 4 * tt, HG_WIDTH), F32),
                        pltpu.VMEM((tt, HG_WIDTH), F32)],
        compiler_params=_params(2),
        name="hgrn",
    )(hz, hz, hz, hz, lb, gn, s0, tri, blk, ones2)


def _softmax_update(s, kt, m_scr, acc_scr):
    m_prev = m_scr[...]
    m_new = jnp.maximum(m_prev, jnp.max(s, axis=1, keepdims=True))
    alpha = jnp.exp2(m_prev - m_new)
    p = jnp.exp2(s - pltpu.repeat(m_new, s.shape[1] // LANES, axis=1))
    acc_scr[...] = pltpu.repeat(alpha, KCAT // LANES, axis=1) * acc_scr[...] + _dot(p.astype(BF16), kt)
    m_scr[...] = m_new


def _attn_finish(acc, o_ref, tq):
    o = acc[:, :KV_LORA_RANK] / acc[:, ONE_COL:ONE_COL + 1]
    for h in range(MLA_HEADS):
        o_ref[:, h * LANES:(h + 1) * LANES] = o[h * tq:(h + 1) * tq].astype(BF16)


def _attn_prompt_body(q_ref, kmeta_ref, k_ref, o_ref, m_scr, acc_scr, sa_scr, sb_scr, sm_scr, *, tq, tk):
    i = pl.program_id(1)
    rows = MLA_HEADS * tq
    q = q_ref[...].reshape(rows, KCAT)
    m_scr[...] = jnp.full(m_scr.shape, NEG, F32)
    acc_scr[...] = jnp.zeros(acc_scr.shape, F32)
    n_full = (i * tq) // tk
    kmeta = kmeta_ref[...]

    def keys(t):
        return k_ref[pl.ds(pl.multiple_of(t * tk, tk), tk), :]

    def scores(t, dst):
        dst[...] = _dot_nt(q, keys(t))

    def consume(src, t):
        _softmax_update(src[...], keys(t), m_scr, acc_scr)

    def consume_last(src):
        rowq = i * tq + lax.broadcasted_iota(jnp.int32, (rows, LANES), 0) % tq
        limit = (rowq // CHUNK + 1) * CHUNK - n_full * tk
        col = lax.broadcasted_iota(jnp.int32, (rows, tk), 1)
        s_last = jnp.where(col < pltpu.repeat(limit, tk // LANES, axis=1), src[...], NEG)
        lane = lax.broadcasted_iota(jnp.int32, (rows, LANES), 1)
        s_meta = jnp.where(lane < N_META, sm_scr[...], NEG)
        m_prev = m_scr[...]
        m_new = jnp.maximum(m_prev, jnp.maximum(jnp.max(s_last, axis=1, keepdims=True),
                                                jnp.max(s_meta, axis=1, keepdims=True)))
        alpha = jnp.exp2(m_prev - m_new)
        p_last = jnp.exp2(s_last - pltpu.repeat(m_new, tk // LANES, axis=1)).astype(BF16)
        p_meta = jnp.exp2(s_meta - m_new).astype(BF16)
        acc = (pltpu.repeat(alpha, KCAT // LANES, axis=1) * acc_scr[...]
               + _dot(p_last, keys(n_full)) + _dot(p_meta, kmeta))
        _attn_finish(acc, o_ref, tq)

    scores(0, sa_scr)
    sm_scr[...] = _dot_nt(q, kmeta)

    def body(p, carry):
        t0 = 2 * p
        scores(t0 + 1, sb_scr)
        consume(sa_scr, t0)
        scores(t0 + 2, sa_scr)
        consume(sb_scr, t0 + 1)
        return carry

    lax.fori_loop(0, n_full // 2, body, 0)

    @pl.when(n_full % 2 == 1)
    def _():
        scores(n_full, sb_scr)
        consume(sa_scr, n_full - 1)
        consume_last(sb_scr)

    @pl.when(n_full % 2 == 0)
    def _():
        consume_last(sa_scr)


def _attn_prompt(qcat, kmeta, kcat, batch, tq, tk):
    t_all = kcat.shape[0]
    seq = t_all // batch
    nq = seq // tq
    rows = MLA_HEADS * tq
    return pl.pallas_call(
        functools.partial(_attn_prompt_body, tq=tq, tk=tk),
        grid=(batch, nq),
        in_specs=[pl.BlockSpec((MLA_HEADS, tq, KCAT), lambda b, i: (0, b * nq + i, 0)),
                  _const_spec(kmeta.shape),
                  pl.BlockSpec((seq, KCAT), lambda b, i: (b, 0))],
        out_specs=pl.BlockSpec((tq, MLA_HEADS * LANES), lambda b, i: (b * nq + i, 0)),
        out_shape=jax.ShapeDtypeStruct((t_all, MLA_HEADS * LANES), BF16),
        scratch_shapes=[pltpu.VMEM((rows, LANES), F32), pltpu.VMEM((rows, KCAT), F32),
                        pltpu.VMEM((rows, tk), F32), pltpu.VMEM((rows, tk), F32), pltpu.VMEM((rows, LANES), F32)],
        compiler_params=_params(2),
        name="attn_prompt",
    )(qcat, kmeta, kcat)


def _attn_sample_body(q_ref, kc_ref, kn_ref, o_ref, *, tq):
    rows = MLA_HEADS * tq
    q = q_ref[...].reshape(rows, KCAT)
    kc = kc_ref[0]
    kn = kn_ref[...]
    s_c = _dot_nt(q, kc)
    s_n = _dot_nt(q, kn)
    m = jnp.maximum(jnp.max(s_c, axis=1, keepdims=True), jnp.max(s_n, axis=1, keepdims=True))
    acc = _dot(jnp.exp2(s_c - m).astype(BF16), kc) + _dot(jnp.exp2(s_n - m).astype(BF16), kn)
    _attn_finish(acc, o_ref, tq)


def _attn_sample(qcat, kcache, kcat, tq):
    batch = kcache.shape[0]
    t_all = kcat.shape[0]
    return pl.pallas_call(
        functools.partial(_attn_sample_body, tq=tq),
        grid=(batch,),
        in_specs=[pl.BlockSpec((MLA_HEADS, tq, KCAT), lambda b: (0, b, 0)),
                  pl.BlockSpec((1,) + kcache.shape[1:], lambda b: (b, 0, 0)),
                  pl.BlockSpec((tq, KCAT), lambda b: (b, 0))],
        out_specs=pl.BlockSpec((tq, MLA_HEADS * LANES), lambda b: (b, 0)),
        out_shape=jax.ShapeDtypeStruct((t_all, MLA_HEADS * LANES), BF16),
        compiler_params=_params(1),
        name="attn_sample",
    )(qcat, kcache, kcat)


def _outproj_body(x_ref, olat_ref, ohg_ref, wuv_ref, wo_ref, fn_ref, wrh_ref, wrl_ref, br_ref,
                  h1_ref, hn_ref, gate_ref):
    o_mla = _dot(olat_ref[...], wuv_ref[...]).astype(BF16)
    mix = _dot(o_mla, wo_ref[:MLA_WIDTH, :]) + _dot(ohg_ref[...], wo_ref[MLA_WIDTH:, :])
    h1 = x_ref[...] + mix
    h1_ref[...] = h1
    hn = _rms(h1, fn_ref[...])
    hn_hi = hn.astype(BF16)
    hn_ref[...] = hn_hi
    hn_lo = (hn - hn_hi.astype(F32)).astype(BF16)
    wrh = wrh_ref[...]
    logits = _dot(hn_hi, wrh) + _dot(hn_hi, wrl_ref[...]) + _dot(hn_lo, wrh) + br_ref[...]
    lane = lax.broadcasted_iota(jnp.int32, logits.shape, 1)
    gl = jnp.where(lane < N_GROUPS, logits, NEG)
    gmax = jnp.max(gl, axis=1, keepdims=True)
    g_top = 1.0 / jnp.sum(jnp.exp(gl - gmax), axis=1, keepdims=True)
    g_idx = jnp.min(jnp.where(gl == gmax, lane, LANES), axis=1, keepdims=True)
    lo = GATE_LANE0 + EXPERTS_PER_GROUP * g_idx
    el = jnp.where((lane >= lo) & (lane < lo + EXPERTS_PER_GROUP), logits, NEG)
    e1 = jnp.max(el, axis=1, keepdims=True)
    i1 = jnp.min(jnp.where(el == e1, lane, LANES), axis=1, keepdims=True)
    el2 = jnp.where(lane == i1, NEG, el)
    e2 = jnp.max(el2, axis=1, keepdims=True)
    i2 = jnp.min(jnp.where(el2 == e2, lane, LANES), axis=1, keepdims=True)
    esum = jnp.sum(jnp.exp(el - e1), axis=1, keepdims=True)
    p1 = 1.0 / esum
    p2 = jnp.exp(e2 - e1) / esum
    w1 = p1 / (p1 + p2) * g_top
    w2 = p2 / (p1 + p2) * g_top
    gate_ref[...] = jnp.where(lane == i1, w1, jnp.where(lane == i2, w2, 0.0))


def _outproj(x2d, olat, ohg, wts, tm):
    t = x2d.shape[0]
    row = lambda w: pl.BlockSpec((tm, w), lambda i: (i, 0))
    return pl.pallas_call(
        _outproj_body,
        grid=(t // tm,),
        in_specs=[row(D_MODEL), row(MLA_HEADS * LANES), row(HG_WIDTH)] + [_const_spec(w.shape) for w in wts],
        out_specs=[row(D_MODEL), row(D_MODEL), row(LANES)],
        out_shape=[jax.ShapeDtypeStruct((t, D_MODEL), F32), jax.ShapeDtypeStruct((t, D_MODEL), BF16),
                   jax.ShapeDtypeStruct((t, LANES), F32)],
        compiler_params=_params(1),
        name="outproj",
    )(x2d, olat, ohg, *wts)


def _moe_body(hn_ref, h1_ref, gate_ref, w1_ref, w3_ref, w2_ref, fin_ref, y_ref):
    t = hn_ref[...]
    gates = gate_ref[...]
    acc = jnp.zeros(h1_ref.shape, F32)
    for grp in range(N_GROUPS):
        parts = []
        for j in range(EXPERTS_PER_GROUP):
            e = grp * EXPERTS_PER_GROUP + j
            a = _dot(t, w1_ref[e])
            c = _dot(t, w3_ref[e])
            lane = GATE_LANE0 + e
            parts.append((a * jax.nn.sigmoid(a) * c * gates[:, lane:lane + 1]).astype(BF16))
        acc = acc + _dot(jnp.concatenate(parts, axis=-1), w2_ref[grp])
    y_ref[...] = _rms(h1_ref[...] + acc, fin_ref[...])


def _moe(hn, h1, gates, w1g, w3g, w2g, fin, tm):
    t = hn.shape[0]
    row = lambda w: pl.BlockSpec((tm, w), lambda i: (i, 0))
    resident = lambda a: pl.BlockSpec(a.shape, lambda i: (0,) * a.ndim, pipeline_mode=pl.Buffered(1))
    return pl.pallas_call(
        _moe_body,
        grid=(t // tm,),
        in_specs=[row(D_MODEL), row(D_MODEL), row(LANES), resident(w1g), resident(w3g), resident(w2g),
                  _const_spec(fin.shape)],
        out_specs=row(D_MODEL),
        out_shape=jax.ShapeDtypeStruct((t, D_MODEL), F32),
        compiler_params=_params(1),
        name="moe",
    )(hn, h1, gates, w1g, w3g, w2g, fin)


def _rope_tables(pos):
    half = ROPE_DIM // 2
    inv = ROPE_THETA ** (-jnp.arange(half, dtype=F32) / half)
    ang = pos.astype(F32)[:, None] * inv[None, :]
    cos, sin = jnp.cos(ang), jnp.sin(ang)
    n = pos.shape[0]
    z = lambda w: jnp.zeros((n, w), F32)
    cosk = jnp.concatenate([cos, cos, z(LANES - ROPE_DIM)], axis=1)
    sink = jnp.concatenate([-sin, sin, z(LANES - ROPE_DIM)], axis=1)
    cosq = jnp.concatenate([jnp.ones((n, QK_NOPE_DIM), F32), cos, cos, z(ROPE_DIM)], axis=1)
    sinq = jnp.concatenate([z(QK_NOPE_DIM), -sin, sin, z(ROPE_DIM)], axis=1)
    return cosk, sink, cosq, sinq


def kernel(x_prompt, x_sample, cache_ckv, cache_krope, state_hgrn, meta_tokens, attn_norm, w_in, q_norm, kv_norm, w_uq, w_uk, w_uv, hg_lb_logits, hg_gnorm, w_o, ffn_norm, w_router_group, b_router_group, w_router_expert, b_router_expert, w1, w3, w2, final_norm):
    bp, seq, _ = x_prompt.shape
    bs, dseq, _ = x_sample.shape
    half = ROPE_DIM // 2
    swap = np.concatenate([np.arange(half, ROPE_DIM), np.arange(half)])

    lb = jax.nn.softmax(hg_lb_logits.astype(F32), axis=0)[0][None, :]
    wi = w_in[0]
    o_kv, o_kr, o_hg = Q_LORA_RANK, Q_LORA_RANK + KV_LORA_RANK, Q_LORA_RANK + KV_LORA_RANK + ROPE_DIM
    w_kr = wi[:, o_kr:o_hg]
    win_ext = jnp.concatenate([wi[:, :o_kr], w_kr, w_kr[:, swap], jnp.zeros((D_MODEL, LANES - 2 * ROPE_DIM), F32),
                               wi[:, o_hg:]], axis=1).astype(BF16)
    wq = w_uq[0]
    wq_rope = wq[:, :, QK_NOPE_DIM:]
    wuq_ext = jnp.concatenate([wq, wq_rope[:, :, swap]], axis=2).reshape(Q_LORA_RANK, MLA_HEADS * LANES).astype(BF16)
    eye = jnp.eye(ROPE_DIM, dtype=F32)
    w2q = jnp.zeros((MLA_HEADS, LANES, KCAT), F32)
    w2q = w2q.at[:, :QK_NOPE_DIM, :KV_LORA_RANK].set(jnp.transpose(w_uk[0], (1, 2, 0)))
    w2q = w2q.at[:, QK_NOPE_DIM:QK_NOPE_DIM + ROPE_DIM, KV_LORA_RANK:KV_LORA_RANK + ROPE_DIM].set(eye[None])
    w2q = w2q.astype(BF16)
    wuv_bd = jnp.zeros((MLA_HEADS, KV_LORA_RANK, MLA_HEADS, V_HEAD_DIM), F32)
    for h in range(MLA_HEADS):
        wuv_bd = wuv_bd.at[h, :, h, :].set(w_uv[0][:, h, :])
    wuv_bd = wuv_bd.reshape(MLA_HEADS * KV_LORA_RANK, MLA_WIDTH).astype(BF16)
    wo = w_o[0].astype(BF16)
    wr = jnp.concatenate([w_router_group[0], w_router_expert[0],
                          jnp.zeros((D_MODEL, LANES - N_GROUPS - N_EXPERTS), F32)], axis=1)
    wr_hi = wr.astype(BF16)
    wr_lo = (wr - wr_hi.astype(F32)).astype(BF16)
    br = jnp.concatenate([b_router_group[0], b_router_expert[0],
                          jnp.zeros((LANES - N_GROUPS - N_EXPERTS,), F32)])[None, :].astype(F32)
    w1g = w1[0].astype(BF16)
    w3g = w3[0].astype(BF16)
    w2g = w2[0].astype(BF16).reshape(N_GROUPS, EXPERTS_PER_GROUP * EXPERT_FF, D_MODEL)
    in_wts = (attn_norm[0][None, :], win_ext, q_norm[0][None, :], kv_norm[0][None, :], wuq_ext, w2q)
    out_wts = (wuv_bd, wo, ffn_norm[0][None, :], wr_hi, wr_lo, br)
    gn = hg_gnorm[0][None, :]
    fin = final_norm[None, :]

    tm_s = 256
    tab_meta = _rope_tables(jnp.arange(N_META))
    tab_p = _rope_tables(N_META + jnp.arange(seq))
    tab_s = tuple(jnp.tile(a, (tm_s // dseq, 1)) for a in _rope_tables(PAST_LEN + jnp.arange(dseq)))
    ckv_m, kr_m, kcat_m, _, hz_m = _inproj(meta_tokens, tab_meta, in_wts, N_META)
    xp = x_prompt.reshape(bp * seq, D_MODEL)
    xs = x_sample.reshape(bs * dseq, D_MODEL)
    ckv_p, kr_p, kcat_p, qcat_p, hz_p = _inproj(xp, tab_p, in_wts, 512)
    ckv_s, kr_s, kcat_s, qcat_s, hz_s = _inproj(xs, tab_s, in_wts, tm_s)

    zero_state = jnp.zeros((1, HG_HEADS, HG_KEY_DIM, HG_VAL_DIM), F32)
    _, st_m = _hgrn(hz_m, lb, gn, zero_state, 1, N_META)
    ohg_p, st_p = _hgrn(hz_p, lb, gn, st_m, bp, 128)
    ohg_s, st_s = _hgrn(hz_s, lb, gn, state_hgrn[0].astype(F32), bs, dseq)

    kmeta = jnp.concatenate([kcat_m, jnp.zeros((LANES - N_META, KCAT), BF16)], axis=0)
    olat_p = _attn_prompt(qcat_p, kmeta, kcat_p, bp, 128, 512)
    n_c = cache_ckv.shape[2]
    kcache = jnp.concatenate([cache_ckv[0], cache_krope[0], jnp.ones((bs, n_c, 1), F32),
                              jnp.zeros((bs, n_c, KCAT - ONE_COL - 1), F32)], axis=-1).astype(BF16)
    olat_s = _attn_sample(qcat_s, kcache, kcat_s, dseq)

    h1_p, hn_p, gate_p = _outproj(xp, olat_p, ohg_p, out_wts, 256)
    h1_s, hn_s, gate_s = _outproj(xs, olat_s, ohg_s, out_wts, 256)
    y_p = _moe(hn_p, h1_p, gate_p, w1g, w3g, w2g, fin, 256)
    y_s = _moe(hn_s, h1_s, gate_s, w1g, w3g, w2g, fin, 256)

    with_meta = lambda m, a, w: jnp.concatenate(
        [jnp.broadcast_to(m[None], (bp, N_META, w)), a.reshape(bp, seq, w)], axis=1)[None]
    return (y_p.reshape(bp, seq, D_MODEL), y_s.reshape(bs, dseq, D_MODEL),
            with_meta(ckv_m, ckv_p, KV_LORA_RANK), with_meta(kr_m, kr_p, ROPE_DIM), st_p[None],
            ckv_s.reshape(1, bs, dseq, KV_LORA_RANK), kr_s.reshape(1, bs, dseq, ROPE_DIM), st_s[None])
```

```python
import functools

import numpy as np
import jax
import jax.numpy as jnp
from jax import lax
from jax.experimental import pallas as pl
from jax.experimental.pallas import tpu as pltpu

F32 = jnp.float32
BF16 = jnp.bfloat16

D_MODEL = 1024
PAST_LEN = 2048
CHUNK = 64
N_META = 16
EPS = 1e-6
MLA_HEADS = 8
Q_LORA_RANK = 256
KV_LORA_RANK = 128
QK_NOPE_DIM = 64
ROPE_DIM = 32
V_HEAD_DIM = 64
ROPE_THETA = 10000.0
MLA_WIDTH = MLA_HEADS * V_HEAD_DIM
SM_SCALE = (QK_NOPE_DIM + ROPE_DIM) ** -0.5
LOG2E = 1.4426950408889634
Q_SCALE = SM_SCALE * LOG2E
HG_HEADS = 4
HG_KEY_DIM = 128
HG_VAL_DIM = 128
HG_WIDTH = HG_HEADS * HG_VAL_DIM
HG_BLOCK = 16
N_GROUPS = 4
EXPERTS_PER_GROUP = 4
N_EXPERTS = N_GROUPS * EXPERTS_PER_GROUP
EXPERT_FF = 256

LANES = 128
KCAT = 2 * LANES
ONE_COL = LANES + ROPE_DIM
NEG = -1e30
VMEM_LIMIT = 56 * 1024 * 1024
GATE_LANE0 = N_GROUPS


def _rms(x, w):
    return x * lax.rsqrt(jnp.mean(x * x, axis=-1, keepdims=True) + EPS) * w


def _dot(a, b):
    return jnp.dot(a, b, preferred_element_type=F32)


def _dot_nt(a, b):
    return lax.dot_general(a, b, (((1,), (1,)), ((), ())), preferred_element_type=F32)


def _dot_tn(a, b):
    return lax.dot_general(a, b, (((0,), (0,)), ((), ())), preferred_element_type=F32)


def _split3(x):
    x1 = x.astype(BF16)
    r = x - x1.astype(F32)
    x2 = r.astype(BF16)
    x3 = (r - x2.astype(F32)).astype(BF16)
    return x1, x2, x3


def _params(n_axes):
    return pltpu.CompilerParams(dimension_semantics=("arbitrary",) * n_axes, vmem_limit_bytes=VMEM_LIMIT)


def _const_spec(shape):
    nd = len(shape)
    return pl.BlockSpec(shape, lambda *_: (0,) * nd)


def _inproj_body(x_ref, cosk_ref, sink_ref, cosq_ref, sinq_ref, an_ref, win_ref, qn_ref, kvn_ref, wuq_ref, w2_ref,
                 ckv_ref, krope_ref, kcat_ref, qcat_ref, hz_ref):
    x = x_ref[...]
    hn = _rms(x, an_ref[...]).astype(BF16)
    z = _dot(hn, win_ref[...])
    hz_ref[...] = z[:, 4 * LANES:]
    ckv = _rms(z[:, 2 * LANES:3 * LANES], kvn_ref[...])
    ckv_ref[...] = ckv
    kr = z[:, 3 * LANES:4 * LANES]
    rot = kr * cosk_ref[...] + pltpu.roll(kr, LANES - ROPE_DIM, 1) * sink_ref[...]
    krope_ref[...] = rot[:, :ROPE_DIM]
    lane = lax.broadcasted_iota(jnp.int32, rot.shape, 1)
    kcat_ref[...] = jnp.concatenate([ckv, jnp.where(lane == ROPE_DIM, 1.0, rot)], axis=-1).astype(BF16)
    cqn = _rms(z[:, :Q_LORA_RANK], qn_ref[...]).astype(BF16)
    qz = _dot(cqn, wuq_ref[...])
    cosq = cosq_ref[...]
    sinq = sinq_ref[...]
    for h in range(MLA_HEADS):
        blk = qz[:, h * LANES:(h + 1) * LANES]
        r = blk * cosq + pltpu.roll(blk, LANES - ROPE_DIM, 1) * sinq
        qcat_ref[h] = (_dot(r.astype(BF16), w2_ref[h]) * Q_SCALE).astype(BF16)


def _inproj(x2d, tabs, wts, tm):
    t = x2d.shape[0]
    period = tabs[0].shape[0] // tm
    grid = (t // tm,)
    row = lambda w: pl.BlockSpec((tm, w), lambda i: (i, 0))
    tab = pl.BlockSpec((tm, LANES), lambda i: (i % period, 0))
    an, win, qn, kvn, wuq, w2 = wts
    return pl.pallas_call(
        _inproj_body,
        grid=grid,
        in_specs=[row(D_MODEL), tab, tab, tab, tab, _const_spec(an.shape), _const_spec(win.shape),
                  _const_spec(qn.shape), _const_spec(kvn.shape), _const_spec(wuq.shape), _const_spec(w2.shape)],
        out_specs=[row(KV_LORA_RANK), row(ROPE_DIM), row(KCAT),
                   pl.BlockSpec((MLA_HEADS, tm, KCAT), lambda i: (0, i, 0)), row(4 * HG_WIDTH)],
        out_shape=[jax.ShapeDtypeStruct((t, KV_LORA_RANK), F32), jax.ShapeDtypeStruct((t, ROPE_DIM), F32),
                   jax.ShapeDtypeStruct((t, KCAT), BF16), jax.ShapeDtypeStruct((MLA_HEADS, t, KCAT), BF16),
                   jax.ShapeDtypeStruct((t, 4 * HG_WIDTH), F32)],
        compiler_params=_params(1),
        name="inproj",
    )(x2d, *tabs, an, win, qn, kvn, wuq, w2)


def _hgrn_body(hq_ref, hf_ref, hi_ref, hg_ref, lb_ref, gn_ref, s0_ref, tri_ref, lv_ref, cst_ref,
               o_ref, sout_ref, st_ref, *, tt):
    step = pl.program_id(1)
    n_levels = tt.bit_length() - 1

    @pl.when(step == 0)
    def _():
        for h in range(HG_HEADS):
            st_ref[h] = s0_ref[0, h].T

    lb = lb_ref[...]
    hq = hq_ref[...]
    q = hq * jax.nn.sigmoid(hq) * (HG_KEY_DIM ** -0.5)
    f = lb + (1.0 - lb) * jax.nn.sigmoid(hf_ref[...])
    g = jnp.log(f) * LOG2E
    kk = 1.0 - f
    v = hi_ref[...]
    g1, g2, g3 = _split3(g)
    tri = tri_ref[...]
    bc = _dot(tri, g1) + _dot(tri, g2) + _dot(tri, g3)

    lv = lv_ref[...]
    bl = bc[tt - 1:tt, :]
    qs = (q * jnp.exp2(bc)).astype(BF16)
    kf = (kk * jnp.exp2(bl - bc)).astype(BF16)
    vb = v.astype(BF16)
    dl = jnp.exp2(bl)
    hg = hg_ref[...]
    gn = gn_ref[...]
    for h in range(HG_HEADS):
        cols = slice(h * LANES, (h + 1) * LANES)
        qh, kh, gh, bch = q[:, cols], kk[:, cols], g[:, cols], bc[:, cols]
        pairs = [(qh, kh)]
        pairs.append((qh * jnp.exp2(gh + cst_ref[0]), kh * cst_ref[1]))
        if n_levels > 1:
            g_prev = pltpu.roll(gh, 1, 0)
            g_next = pltpu.roll(gh, tt - 1, 0)
            pairs.append((qh * jnp.exp2(gh + g_prev * cst_ref[2] + cst_ref[3]),
                          kh * jnp.exp2(g_next * cst_ref[4] + cst_ref[5])))
        for lvl in range(2, n_levels):
            grp = 2 << lvl
            mid = bch.reshape(tt // grp, grp, LANES)[:, grp // 2 - 1:grp // 2, :]
            d = bch - jnp.broadcast_to(mid, (tt // grp, grp, LANES)).reshape(tt, LANES)
            pairs.append((qh * jnp.exp2(d + cst_ref[2 + 2 * lvl]), kh * jnp.exp2(cst_ref[3 + 2 * lvl] - d)))
        att = jnp.zeros((tt, tt), F32)
        for idx, (qa, ka) in enumerate(pairs):
            att = jnp.where(lv == idx - 1, _dot_nt(qa.astype(BF16), ka.astype(BF16)), att)
        st = st_ref[h]
        o = _dot(att.astype(BF16), vb[:, cols]) + _dot_nt(qs[:, cols], st.astype(BF16))
        st_ref[h] = dl[:, cols] * st + _dot_tn(vb[:, cols], kf[:, cols])
        gate = hg[:, cols]
        o_ref[:, cols] = (_rms(o, gn) * (gate * jax.nn.sigmoid(gate))).astype(BF16)

    @pl.when(step == pl.num_programs(1) - 1)
    def _():
        for h in range(HG_HEADS):
            sout_ref[0, h] = st_ref[h].T


def _hgrn(hz, lb, gn, s0, batch, tt):
    t_all = hz.shape[0]
    steps = t_all // batch // tt
    idx = np.arange(tt)
    tri = jnp.asarray(idx[None, :] <= idx[:, None], BF16)
    diff = idx[:, None] ^ idx[None, :]
    level = np.where(diff > 0, np.floor(np.log2(np.maximum(diff, 1))), -1).astype(np.int32)
    lv = jnp.asarray(np.where(idx[None, :] <= idx[:, None], level, tt), jnp.int32)
    rows = np.broadcast_to(idx[:, None], (tt, LANES))
    bias = lambda keep: np.where(keep, 0.0, NEG)
    one = lambda keep: np.where(keep, 1.0, 0.0)
    cst = [bias(rows % 2 == 1), one(rows % 2 == 0),
           one(rows % 4 == 3), bias(rows % 4 >= 2), one(rows % 4 == 0), bias(rows % 4 < 2)]
    for lvl in range(2, tt.bit_length() - 1):
        upper = (rows >> lvl) % 2 == 1
        cst += [bias(upper), bias(~upper)]
    cst = jnp.asarray(np.stack(cst), F32)
    col = lambda c: pl.BlockSpec((tt, HG_WIDTH), lambda b, s: (b * steps + s, c))
    shared_s0 = s0.shape[0] == 1
    state = lambda shared: pl.BlockSpec((1, HG_HEADS, HG_KEY_DIM, HG_VAL_DIM),
                                        (lambda b, s: (0, 0, 0, 0)) if shared else (lambda b, s: (b, 0, 0, 0)))
    return pl.pallas_call(
        functools.partial(_hgrn_body, tt=tt),
        grid=(batch, steps),
        in_specs=[col(0), col(1), col(2), col(3), _const_spec(lb.shape), _const_spec(gn.shape), state(shared_s0),
                  _const_spec(tri.shape), _const_spec(lv.shape), _const_spec(cst.shape)],
        out_specs=[pl.BlockSpec((tt, HG_WIDTH), lambda b, s: (b * steps + s, 0)), state(False)],
        out_shape=[jax.ShapeDtypeStruct((t_all, HG_WIDTH), BF16),
                   jax.ShapeDtypeStruct((batch, HG_HEADS, HG_KEY_DIM, HG_VAL_DIM), F32)],
        scratch_shapes=[pltpu.VMEM((HG_HEADS, HG_VAL_DIM, HG_KEY_DIM), F32)],
        compiler_params=_params(2),
        name="hgrn",
    )(hz, hz, hz, hz, lb, gn, s0, tri, lv, cst)


def _lanes(x, n):
    return jnp.concatenate([x] * n, axis=1)


def _softmax_update(s, kt, m_scr, acc_scr):
    m_prev = m_scr[...]
    m_new = jnp.maximum(m_prev, jnp.max(s, axis=1, keepdims=True))
    alpha = jnp.exp2(m_prev - m_new)
    p = jnp.exp2(s - _lanes(m_new, s.shape[1] // LANES))
    acc_scr[...] = _lanes(alpha, KCAT // LANES) * acc_scr[...] + _dot(p.astype(BF16), kt)
    m_scr[...] = m_new


def _attn_finish(acc, o_ref, tq):
    o = acc[:, :KV_LORA_RANK] / acc[:, ONE_COL:ONE_COL + 1]
    for h in range(MLA_HEADS):
        o_ref[:, h * LANES:(h + 1) * LANES] = o[h * tq:(h + 1) * tq].astype(BF16)


def _attn_prompt_body(q_ref, kmeta_ref, k_ref, o_ref, m_scr, acc_scr, sa_scr, sb_scr, sm_scr, *, tq, tk):
    i = pl.program_id(1)
    rows = MLA_HEADS * tq
    q = q_ref[...].reshape(rows, KCAT)
    m_scr[...] = jnp.full(m_scr.shape, NEG, F32)
    acc_scr[...] = jnp.zeros(acc_scr.shape, F32)
    n_full = (i * tq) // tk
    kmeta = kmeta_ref[...]

    def keys(t):
        return k_ref[pl.ds(pl.multiple_of(t * tk, tk), tk), :]

    def scores(t, dst):
        dst[...] = _dot_nt(q, keys(t))

    def consume(src, t):
        _softmax_update(src[...], keys(t), m_scr, acc_scr)

    def consume_last(src):
        rowq = i * tq + lax.broadcasted_iota(jnp.int32, (rows, LANES), 0) % tq
        limit = (rowq // CHUNK + 1) * CHUNK - n_full * tk
        col = lax.broadcasted_iota(jnp.int32, (rows, tk), 1)
        s_last = jnp.where(col < _lanes(limit, tk // LANES), src[...], NEG)
        lane = lax.broadcasted_iota(jnp.int32, (rows, LANES), 1)
        s_meta = jnp.where(lane < N_META, sm_scr[...], NEG)
        m_prev = m_scr[...]
        m_new = jnp.maximum(m_prev, jnp.maximum(jnp.max(s_last, axis=1, keepdims=True),
                                                jnp.max(s_meta, axis=1, keepdims=True)))
        alpha = jnp.exp2(m_prev - m_new)
        p_last = jnp.exp2(s_last - _lanes(m_new, tk // LANES)).astype(BF16)
        p_meta = jnp.exp2(s_meta - m_new).astype(BF16)
        acc = (_lanes(alpha, KCAT // LANES) * acc_scr[...]
               + _dot(p_last, keys(n_full)) + _dot(p_meta, kmeta))
        _attn_finish(acc, o_ref, tq)

    scores(0, sa_scr)
    sm_scr[...] = _dot_nt(q, kmeta)

    def body(p, carry):
        t0 = 2 * p
        scores(t0 + 1, sb_scr)
        consume(sa_scr, t0)
        scores(t0 + 2, sa_scr)
        consume(sb_scr, t0 + 1)
        return carry

    lax.fori_loop(0, n_full // 2, body, 0)

    @pl.when(n_full % 2 == 1)
    def _():
        scores(n_full, sb_scr)
        consume(sa_scr, n_full - 1)
        consume_last(sb_scr)

    @pl.when(n_full % 2 == 0)
    def _():
        consume_last(sa_scr)


def _attn_prompt(qcat, kmeta, kcat, batch, tq, tk):
    t_all = kcat.shape[0]
    seq = t_all // batch
    nq = seq // tq
    rows = MLA_HEADS * tq
    return pl.pallas_call(
        functools.partial(_attn_prompt_body, tq=tq, tk=tk),
        grid=(batch, nq),
        in_specs=[pl.BlockSpec((MLA_HEADS, tq, KCAT), lambda b, i: (0, b * nq + i, 0)),
                  _const_spec(kmeta.shape),
                  pl.BlockSpec((seq, KCAT), lambda b, i: (b, 0))],
        out_specs=pl.BlockSpec((tq, MLA_HEADS * LANES), lambda b, i: (b * nq + i, 0)),
        out_shape=jax.ShapeDtypeStruct((t_all, MLA_HEADS * LANES), BF16),
        scratch_shapes=[pltpu.VMEM((rows, LANES), F32), pltpu.VMEM((rows, KCAT), F32),
                        pltpu.VMEM((rows, tk), F32), pltpu.VMEM((rows, tk), F32), pltpu.VMEM((rows, LANES), F32)],
        compiler_params=_params(2),
        name="attn_prompt",
    )(qcat, kmeta, kcat)


def _attn_sample_body(q_ref, kc_ref, kn_ref, o_ref, *, tq):
    rows = MLA_HEADS * tq
    q = q_ref[...].reshape(rows, KCAT)
    kc = kc_ref[0]
    kn = kn_ref[...]
    s_c = _dot_nt(q, kc)
    s_n = _dot_nt(q, kn)
    m = jnp.maximum(jnp.max(s_c, axis=1, keepdims=True), jnp.max(s_n, axis=1, keepdims=True))
    acc = _dot(jnp.exp2(s_c - m).astype(BF16), kc) + _dot(jnp.exp2(s_n - m).astype(BF16), kn)
    _attn_finish(acc, o_ref, tq)


def _attn_sample(qcat, kcache, kcat, tq):
    batch = kcache.shape[0]
    t_all = kcat.shape[0]
    return pl.pallas_call(
        functools.partial(_attn_sample_body, tq=tq),
        grid=(batch,),
        in_specs=[pl.BlockSpec((MLA_HEADS, tq, KCAT), lambda b: (0, b, 0)),
                  pl.BlockSpec((1,) + kcache.shape[1:], lambda b: (b, 0, 0)),
                  pl.BlockSpec((tq, KCAT), lambda b: (b, 0))],
        out_specs=pl.BlockSpec((tq, MLA_HEADS * LANES), lambda b: (b, 0)),
        out_shape=jax.ShapeDtypeStruct((t_all, MLA_HEADS * LANES), BF16),
        compiler_params=_params(1),
        name="attn_sample",
    )(qcat, kcache, kcat)


def _outproj_body(x_ref, olat_ref, ohg_ref, wuv_ref, wo_ref, fn_ref, wrh_ref, wrl_ref, br_ref,
                  h1_ref, hn_ref, gate_ref):
    o_mla = _dot(olat_ref[...], wuv_ref[...]).astype(BF16)
    mix = _dot(o_mla, wo_ref[:MLA_WIDTH, :]) + _dot(ohg_ref[...], wo_ref[MLA_WIDTH:, :])
    h1 = x_ref[...] + mix
    h1_ref[...] = h1
    hn = _rms(h1, fn_ref[...])
    hn_hi = hn.astype(BF16)
    hn_ref[...] = hn_hi
    hn_lo = (hn - hn_hi.astype(F32)).astype(BF16)
    wrh = wrh_ref[...]
    logits = _dot(hn_hi, wrh) + _dot(hn_hi, wrl_ref[...]) + _dot(hn_lo, wrh) + br_ref[...]
    lane = lax.broadcasted_iota(jnp.int32, logits.shape, 1)
    gl = jnp.where(lane < N_GROUPS, logits, NEG)
    gmax = jnp.max(gl, axis=1, keepdims=True)
    g_top = 1.0 / jnp.sum(jnp.exp(gl - gmax), axis=1, keepdims=True)
    g_idx = jnp.min(jnp.where(gl == gmax, lane, LANES), axis=1, keepdims=True)
    lo = GATE_LANE0 + EXPERTS_PER_GROUP * g_idx
    el = jnp.where((lane >= lo) & (lane < lo + EXPERTS_PER_GROUP), logits, NEG)
    e1 = jnp.max(el, axis=1, keepdims=True)
    i1 = jnp.min(jnp.where(el == e1, lane, LANES), axis=1, keepdims=True)
    el2 = jnp.where(lane == i1, NEG, el)
    e2 = jnp.max(el2, axis=1, keepdims=True)
    i2 = jnp.min(jnp.where(el2 == e2, lane, LANES), axis=1, keepdims=True)
    esum = jnp.sum(jnp.exp(el - e1), axis=1, keepdims=True)
    p1 = 1.0 / esum
    p2 = jnp.exp(e2 - e1) / esum
    w1 = p1 / (p1 + p2) * g_top
    w2 = p2 / (p1 + p2) * g_top
    gate_ref[...] = jnp.where(lane == i1, w1, jnp.where(lane == i2, w2, 0.0))


def _outproj(x2d, olat, ohg, wts, tm):
    t = x2d.shape[0]
    row = lambda w: pl.BlockSpec((tm, w), lambda i: (i, 0))
    return pl.pallas_call(
        _outproj_body,
        grid=(t // tm,),
        in_specs=[row(D_MODEL), row(MLA_HEADS * LANES), row(HG_WIDTH)] + [_const_spec(w.shape) for w in wts],
        out_specs=[row(D_MODEL), row(D_MODEL), row(LANES)],
        out_shape=[jax.ShapeDtypeStruct((t, D_MODEL), F32), jax.ShapeDtypeStruct((t, D_MODEL), BF16),
                   jax.ShapeDtypeStruct((t, LANES), F32)],
        compiler_params=_params(1),
        name="outproj",
    )(x2d, olat, ohg, *wts)


def _moe_body(hn_ref, h1_ref, gate_ref, w1_ref, w3_ref, w2_ref, fin_ref, y_ref):
    t = hn_ref[...]
    gates = gate_ref[...]
    acc = jnp.zeros(h1_ref.shape, F32)
    for grp in range(N_GROUPS):
        parts = []
        for j in range(EXPERTS_PER_GROUP):
            e = grp * EXPERTS_PER_GROUP + j
            a = _dot(t, w1_ref[e])
            c = _dot(t, w3_ref[e])
            lane = GATE_LANE0 + e
            parts.append((a * jax.nn.sigmoid(a) * c * gates[:, lane:lane + 1]).astype(BF16))
        acc = acc + _dot(jnp.concatenate(parts, axis=-1), w2_ref[grp])
    y_ref[...] = _rms(h1_ref[...] + acc, fin_ref[...])


def _moe(hn, h1, gates, w1g, w3g, w2g, fin, tm):
    t = hn.shape[0]
    row = lambda w: pl.BlockSpec((tm, w), lambda i: (i, 0))
    resident = lambda a: pl.BlockSpec(a.shape, lambda i: (0,) * a.ndim, pipeline_mode=pl.Buffered(1))
    return pl.pallas_call(
        _moe_body,
        grid=(t // tm,),
        in_specs=[row(D_MODEL), row(D_MODEL), row(LANES), resident(w1g), resident(w3g), resident(w2g),
                  _const_spec(fin.shape)],
        out_specs=row(D_MODEL),
        out_shape=jax.ShapeDtypeStruct((t, D_MODEL), F32),
        compiler_params=_params(1),
        name="moe",
    )(hn, h1, gates, w1g, w3g, w2g, fin)


def _rope_tables(pos):
    half = ROPE_DIM // 2
    inv = ROPE_THETA ** (-jnp.arange(half, dtype=F32) / half)
    ang = pos.astype(F32)[:, None] * inv[None, :]
    cos, sin = jnp.cos(ang), jnp.sin(ang)
    n = pos.shape[0]
    z = lambda w: jnp.zeros((n, w), F32)
    cosk = jnp.concatenate([cos, cos, z(LANES - ROPE_DIM)], axis=1)
    sink = jnp.concatenate([-sin, sin, z(LANES - ROPE_DIM)], axis=1)
    cosq = jnp.concatenate([jnp.ones((n, QK_NOPE_DIM), F32), cos, cos, z(ROPE_DIM)], axis=1)
    sinq = jnp.concatenate([z(QK_NOPE_DIM), -sin, sin, z(ROPE_DIM)], axis=1)
    return cosk, sink, cosq, sinq


def kernel(x_prompt, x_sample, cache_ckv, cache_krope, state_hgrn, meta_tokens, attn_norm, w_in, q_norm, kv_norm, w_uq, w_uk, w_uv, hg_lb_logits, hg_gnorm, w_o, ffn_norm, w_router_group, b_router_group, w_router_expert, b_router_expert, w1, w3, w2, final_norm):
    bp, seq, _ = x_prompt.shape
    bs, dseq, _ = x_sample.shape
    half = ROPE_DIM // 2
    swap = np.concatenate([np.arange(half, ROPE_DIM), np.arange(half)])

    lb = jax.nn.softmax(hg_lb_logits.astype(F32), axis=0)[0][None, :]
    wi = w_in[0]
    o_kv, o_kr, o_hg = Q_LORA_RANK, Q_LORA_RANK + KV_LORA_RANK, Q_LORA_RANK + KV_LORA_RANK + ROPE_DIM
    w_kr = wi[:, o_kr:o_hg]
    win_ext = jnp.concatenate([wi[:, :o_kr], w_kr, w_kr[:, swap], jnp.zeros((D_MODEL, LANES - 2 * ROPE_DIM), F32),
                               wi[:, o_hg:]], axis=1).astype(BF16)
    wq = w_uq[0]
    wq_rope = wq[:, :, QK_NOPE_DIM:]
    wuq_ext = jnp.concatenate([wq, wq_rope[:, :, swap]], axis=2).reshape(Q_LORA_RANK, MLA_HEADS * LANES).astype(BF16)
    eye = jnp.eye(ROPE_DIM, dtype=F32)
    w2q = jnp.zeros((MLA_HEADS, LANES, KCAT), F32)
    w2q = w2q.at[:, :QK_NOPE_DIM, :KV_LORA_RANK].set(jnp.transpose(w_uk[0], (1, 2, 0)))
    w2q = w2q.at[:, QK_NOPE_DIM:QK_NOPE_DIM + ROPE_DIM, KV_LORA_RANK:KV_LORA_RANK + ROPE_DIM].set(eye[None])
    w2q = w2q.astype(BF16)
    wuv_bd = jnp.zeros((MLA_HEADS, KV_LORA_RANK, MLA_HEADS, V_HEAD_DIM), F32)
    for h in range(MLA_HEADS):
        wuv_bd = wuv_bd.at[h, :, h, :].set(w_uv[0][:, h, :])
    wuv_bd = wuv_bd.reshape(MLA_HEADS * KV_LORA_RANK, MLA_WIDTH).astype(BF16)
    wo = w_o[0].astype(BF16)
    wr = jnp.concatenate([w_router_group[0], w_router_expert[0],
                          jnp.zeros((D_MODEL, LANES - N_GROUPS - N_EXPERTS), F32)], axis=1)
    wr_hi = wr.astype(BF16)
    wr_lo = (wr - wr_hi.astype(F32)).astype(BF16)
    br = jnp.concatenate([b_router_group[0], b_router_expert[0],
                          jnp.zeros((LANES - N_GROUPS - N_EXPERTS,), F32)])[None, :].astype(F32)
    w1g = w1[0].astype(BF16)
    w3g = w3[0].astype(BF16)
    w2g = w2[0].astype(BF16).reshape(N_GROUPS, EXPERTS_PER_GROUP * EXPERT_FF, D_MODEL)
    in_wts = (attn_norm[0][None, :], win_ext, q_norm[0][None, :], kv_norm[0][None, :], wuq_ext, w2q)
    out_wts = (wuv_bd, wo, ffn_norm[0][None, :], wr_hi, wr_lo, br)
    gn = hg_gnorm[0][None, :]
    fin = final_norm[None, :]

    tm_s = 256
    tab_meta = _rope_tables(jnp.arange(N_META))
    tab_p = _rope_tables(N_META + jnp.arange(seq))
    tab_s = tuple(jnp.tile(a, (tm_s // dseq, 1)) for a in _rope_tables(PAST_LEN + jnp.arange(dseq)))
    ckv_m, kr_m, kcat_m, _, hz_m = _inproj(meta_tokens, tab_meta, in_wts, N_META)
    xp = x_prompt.reshape(bp * seq, D_MODEL)
    xs = x_sample.reshape(bs * dseq, D_MODEL)
    ckv_p, kr_p, kcat_p, qcat_p, hz_p = _inproj(xp, tab_p, in_wts, 512)
    ckv_s, kr_s, kcat_s, qcat_s, hz_s = _inproj(xs, tab_s, in_wts, tm_s)

    zero_state = jnp.zeros((1, HG_HEADS, HG_KEY_DIM, HG_VAL_DIM), F32)
    _, st_m = _hgrn(hz_m, lb, gn, zero_state, 1, N_META)
    ohg_p, st_p = _hgrn(hz_p, lb, gn, st_m, bp, 128)
    ohg_s, st_s = _hgrn(hz_s, lb, gn, state_hgrn[0].astype(F32), bs, dseq)

    kmeta = jnp.concatenate([kcat_m, jnp.zeros((LANES - N_META, KCAT), BF16)], axis=0)
    olat_p = _attn_prompt(qcat_p, kmeta, kcat_p, bp, 128, 512)
    n_c = cache_ckv.shape[2]
    kcache = jnp.concatenate([cache_ckv[0], cache_krope[0], jnp.ones((bs, n_c, 1), F32),
                              jnp.zeros((bs, n_c, KCAT - ONE_COL - 1), F32)], axis=-1).astype(BF16)
    olat_s = _attn_sample(qcat_s, kcache, kcat_s, dseq)

    h1_p, hn_p, gate_p = _outproj(xp, olat_p, ohg_p, out_wts, 256)
    h1_s, hn_s, gate_s = _outproj(xs, olat_s, ohg_s, out_wts, 256)
    y_p = _moe(hn_p, h1_p, gate_p, w1g, w3g, w2g, fin, 256)
    y_s = _moe(hn_s, h1_s, gate_s, w1g, w3g, w2g, fin, 256)

    with_meta = lambda m, a, w: jnp.concatenate(
        [jnp.broadcast_to(m[None], (bp, N_META, w)), a.reshape(bp, seq, w)], axis=1)[None]
    return (y_p.reshape(bp, seq, D_MODEL), y_s.reshape(bs, dseq, D_MODEL),
            with_meta(ckv_m, ckv_p, KV_LORA_RANK), with_meta(kr_m, kr_p, ROPE_DIM), st_p[None],
            ckv_s.reshape(1, bs, dseq, KV_LORA_RANK), kr_s.reshape(1, bs, dseq, ROPE_DIM), st_s[None])
```

```python
import functools

import numpy as np
import jax
import jax.numpy as jnp
from jax import lax
from jax.experimental import pallas as pl
from jax.experimental.pallas import tpu as pltpu

F32 = jnp.float32
BF16 = jnp.bfloat16

D_MODEL = 1024
PAST_LEN = 2048
CHUNK = 64
N_META = 16
EPS = 1e-6
MLA_HEADS = 8
Q_LORA_RANK = 256
KV_LORA_RANK = 128
QK_NOPE_DIM = 64
ROPE_DIM = 32
V_HEAD_DIM = 64
ROPE_THETA = 10000.0
MLA_WIDTH = MLA_HEADS * V_HEAD_DIM
SM_SCALE = (QK_NOPE_DIM + ROPE_DIM) ** -0.5
LOG2E = 1.4426950408889634
Q_SCALE = SM_SCALE * LOG2E
HG_HEADS = 4
HG_KEY_DIM = 128
HG_VAL_DIM = 128
HG_WIDTH = HG_HEADS * HG_VAL_DIM
HG_BLOCK = 16
N_GROUPS = 4
EXPERTS_PER_GROUP = 4
N_EXPERTS = N_GROUPS * EXPERTS_PER_GROUP
EXPERT_FF = 256

LANES = 128
KCAT = 2 * LANES
ONE_COL = LANES + ROPE_DIM
NEG = -1e30
VMEM_LIMIT = 56 * 1024 * 1024
GATE_LANE0 = N_GROUPS


def _rms(x, w):
    return x * lax.rsqrt(jnp.mean(x * x, axis=-1, keepdims=True) + EPS) * w


def _dot(a, b):
    return jnp.dot(a, b, preferred_element_type=F32)


def _dot_nt(a, b):
    return lax.dot_general(a, b, (((1,), (1,)), ((), ())), preferred_element_type=F32)


def _dot_tn(a, b):
    return lax.dot_general(a, b, (((0,), (0,)), ((), ())), preferred_element_type=F32)


def _split3(x):
    x1 = x.astype(BF16)
    r = x - x1.astype(F32)
    x2 = r.astype(BF16)
    x3 = (r - x2.astype(F32)).astype(BF16)
    return x1, x2, x3


def _params(n_axes):
    return pltpu.CompilerParams(dimension_semantics=("arbitrary",) * n_axes, vmem_limit_bytes=VMEM_LIMIT)


def _const_spec(shape):
    nd = len(shape)
    return pl.BlockSpec(shape, lambda *_: (0,) * nd)


def _inproj_body(x_ref, cosk_ref, sink_ref, cosq_ref, sinq_ref, an_ref, win_ref, qn_ref, kvn_ref, wuq_ref, w2_ref,
                 ckv_ref, krope_ref, kcat_ref, qcat_ref, hz_ref):
    x = x_ref[...]
    hn = _rms(x, an_ref[...]).astype(BF16)
    z = _dot(hn, win_ref[...])
    hz_ref[...] = z[:, 4 * LANES:]
    ckv = _rms(z[:, 2 * LANES:3 * LANES], kvn_ref[...])
    ckv_ref[...] = ckv
    kr = z[:, 3 * LANES:4 * LANES]
    rot = kr * cosk_ref[...] + pltpu.roll(kr, LANES - ROPE_DIM, 1) * sink_ref[...]
    krope_ref[...] = rot[:, :ROPE_DIM]
    lane = lax.broadcasted_iota(jnp.int32, rot.shape, 1)
    kcat_ref[...] = jnp.concatenate([ckv, jnp.where(lane == ROPE_DIM, 1.0, rot)], axis=-1).astype(BF16)
    cqn = _rms(z[:, :Q_LORA_RANK], qn_ref[...]).astype(BF16)
    qz = _dot(cqn, wuq_ref[...])
    cosq = cosq_ref[...]
    sinq = sinq_ref[...]
    for h in range(MLA_HEADS):
        blk = qz[:, h * LANES:(h + 1) * LANES]
        r = blk * cosq + pltpu.roll(blk, LANES - ROPE_DIM, 1) * sinq
        qcat_ref[h] = (_dot(r.astype(BF16), w2_ref[h]) * Q_SCALE).astype(BF16)


def _inproj(x2d, tabs, wts, tm):
    t = x2d.shape[0]
    period = tabs[0].shape[0] // tm
    grid = (t // tm,)
    row = lambda w: pl.BlockSpec((tm, w), lambda i: (i, 0))
    tab = pl.BlockSpec((tm, LANES), lambda i: (i % period, 0))
    an, win, qn, kvn, wuq, w2 = wts
    return pl.pallas_call(
        _inproj_body,
        grid=grid,
        in_specs=[row(D_MODEL), tab, tab, tab, tab, _const_spec(an.shape), _const_spec(win.shape),
                  _const_spec(qn.shape), _const_spec(kvn.shape), _const_spec(wuq.shape), _const_spec(w2.shape)],
        out_specs=[row(KV_LORA_RANK), row(ROPE_DIM), row(KCAT),
                   pl.BlockSpec((MLA_HEADS, tm, KCAT), lambda i: (0, i, 0)), row(4 * HG_WIDTH)],
        out_shape=[jax.ShapeDtypeStruct((t, KV_LORA_RANK), F32), jax.ShapeDtypeStruct((t, ROPE_DIM), F32),
                   jax.ShapeDtypeStruct((t, KCAT), BF16), jax.ShapeDtypeStruct((MLA_HEADS, t, KCAT), BF16),
                   jax.ShapeDtypeStruct((t, 4 * HG_WIDTH), F32)],
        compiler_params=_params(1),
        name="inproj",
    )(x2d, *tabs, an, win, qn, kvn, wuq, w2)


def _hgrn_body(hq_ref, hf_ref, hi_ref, hg_ref, lb_ref, gn_ref, s0_ref, tri_ref, lv_ref, cst_ref,
               o_ref, sout_ref, st_ref, *, tt):
    step = pl.program_id(1)
    n_levels = tt.bit_length() - 1

    @pl.when(step == 0)
    def _():
        for h in range(HG_HEADS):
            st_ref[h] = s0_ref[0, h].T

    lb = lb_ref[...]
    hq = hq_ref[...]
    q = hq * jax.nn.sigmoid(hq) * (HG_KEY_DIM ** -0.5)
    f = lb + (1.0 - lb) * jax.nn.sigmoid(hf_ref[...])
    g = jnp.log(f) * LOG2E
    kk = 1.0 - f
    v = hi_ref[...]
    g1, g2, g3 = _split3(g)
    tri = tri_ref[...]
    bc = _dot(tri, g1) + _dot(tri, g2) + _dot(tri, g3)

    lv = lv_ref[...]
    bl = bc[tt - 1:tt, :]
    qs = (q * jnp.exp2(bc)).astype(BF16)
    kf = (kk * jnp.exp2(bl - bc)).astype(BF16)
    vb = v.astype(BF16)
    dl = jnp.exp2(bl)
    hg = hg_ref[...]
    gn = gn_ref[...]
    for h in range(HG_HEADS):
        cols = slice(h * LANES, (h + 1) * LANES)
        qh, kh, gh, bch = q[:, cols], kk[:, cols], g[:, cols], bc[:, cols]
        pairs = [(qh, kh)]
        pairs.append((qh * jnp.exp2(gh + cst_ref[0]), kh * cst_ref[1]))
        if n_levels > 1:
            g_prev = pltpu.roll(gh, 1, 0)
            g_next = pltpu.roll(gh, tt - 1, 0)
            pairs.append((qh * jnp.exp2(gh + g_prev * cst_ref[2] + cst_ref[3]),
                          kh * jnp.exp2(g_next * cst_ref[4] + cst_ref[5])))
        for lvl in range(2, n_levels):
            grp = 2 << lvl
            mid = bch.reshape(tt // grp, grp, LANES)[:, grp // 2 - 1:grp // 2, :]
            d = bch - jnp.broadcast_to(mid, (tt // grp, grp, LANES)).reshape(tt, LANES)
            pairs.append((qh * jnp.exp2(d + cst_ref[2 + 2 * lvl]), kh * jnp.exp2(cst_ref[3 + 2 * lvl] - d)))
        att = jnp.zeros((tt, tt), F32)
        for idx, (qa, ka) in enumerate(pairs):
            att = jnp.where(lv == idx - 1, _dot_nt(qa.astype(BF16), ka.astype(BF16)), att)
        st = st_ref[h]
        o = _dot(att.astype(BF16), vb[:, cols]) + _dot_nt(qs[:, cols], st.astype(BF16))
        st_ref[h] = dl[:, cols] * st + _dot_tn(vb[:, cols], kf[:, cols])
        gate = hg[:, cols]
        o_ref[:, cols] = (_rms(o, gn) * (gate * jax.nn.sigmoid(gate))).astype(BF16)

    @pl.when(step == pl.num_programs(1) - 1)
    def _():
        for h in range(HG_HEADS):
            sout_ref[0, h] = st_ref[h].T


def _hgrn(hz, lb, gn, s0, batch, tt):
    t_all = hz.shape[0]
    steps = t_all // batch // tt
    idx = np.arange(tt)
    tri = jnp.asarray(idx[None, :] <= idx[:, None], BF16)
    diff = idx[:, None] ^ idx[None, :]
    level = np.where(diff > 0, np.floor(np.log2(np.maximum(diff, 1))), -1).astype(np.int32)
    lv = jnp.asarray(np.where(idx[None, :] <= idx[:, None], level, tt), jnp.int32)
    rows = np.broadcast_to(idx[:, None], (tt, LANES))
    bias = lambda keep: np.where(keep, 0.0, NEG)
    one = lambda keep: np.where(keep, 1.0, 0.0)
    cst = [bias(rows % 2 == 1), one(rows % 2 == 0),
           one(rows % 4 == 3), bias(rows % 4 >= 2), one(rows % 4 == 0), bias(rows % 4 < 2)]
    for lvl in range(2, tt.bit_length() - 1):
        upper = (rows >> lvl) % 2 == 1
        cst += [bias(upper), bias(~upper)]
    cst = jnp.asarray(np.stack(cst), F32)
    col = lambda c: pl.BlockSpec((tt, HG_WIDTH), lambda b, s: (b * steps + s, c))
    shared_s0 = s0.shape[0] == 1
    state = lambda shared: pl.BlockSpec((1, HG_HEADS, HG_KEY_DIM, HG_VAL_DIM),
                                        (lambda b, s: (0, 0, 0, 0)) if shared else (lambda b, s: (b, 0, 0, 0)))
    return pl.pallas_call(
        functools.partial(_hgrn_body, tt=tt),
        grid=(batch, steps),
        in_specs=[col(0), col(1), col(2), col(3), _const_spec(lb.shape), _const_spec(gn.shape), state(shared_s0),
                  _const_spec(tri.shape), _const_spec(lv.shape), _const_spec(cst.shape)],
        out_specs=[pl.BlockSpec((tt, HG_WIDTH), lambda b, s: (b * steps + s, 0)), state(False)],
        out_shape=[jax.ShapeDtypeStruct((t_all, HG_WIDTH), BF16),
                   jax.ShapeDtypeStruct((batch, HG_HEADS, HG_KEY_DIM, HG_VAL_DIM), F32)],
        scratch_shapes=[pltpu.VMEM((HG_HEADS, HG_VAL_DIM, HG_KEY_DIM), F32)],
        compiler_params=_params(2),
        name="hgrn",
    )(hz, hz, hz, hz, lb, gn, s0, tri, lv, cst)


def _lanes(x, n):
    return jnp.concatenate([x] * n, axis=1)


def _softmax_update(s, kt, m_scr, acc_scr):
    m_prev = m_scr[...]
    m_new = jnp.maximum(m_prev, jnp.max(s, axis=1, keepdims=True))
    alpha = jnp.exp2(m_prev - m_new)
    p = jnp.exp2(s - _lanes(m_new, s.shape[1] // LANES))
    acc_scr[...] = _lanes(alpha, KCAT // LANES) * acc_scr[...] + _dot(p.astype(BF16), kt)
    m_scr[...] = m_new


def _attn_finish(acc, o_ref, tq, row0=0):
    o = acc[:, :KV_LORA_RANK] / acc[:, ONE_COL:ONE_COL + 1]
    for h in range(MLA_HEADS):
        o_ref[row0:row0 + tq, h * LANES:(h + 1) * LANES] = o[h * tq:(h + 1) * tq].astype(BF16)


def _attn_prompt_body(q_ref, qn_ref, kmeta_ref, k_ref, o_ref, m_scr, acc_scr, sl_scr, sm_scr, sa_scr, sb_scr,
                      *, tq, tk):
    i = pl.program_id(1)
    rows = MLA_HEADS * tq
    n_tiles = k_ref.shape[0] // tk
    q = q_ref[...].reshape(rows, KCAT)
    kmeta = kmeta_ref[...]
    n = (i * tq) // tk

    def keys(t):
        return k_ref[pl.ds(pl.multiple_of(t * tk, tk), tk), :]

    def scores(qv, t, dst):
        dst[...] = _dot_nt(qv, keys(t))

    def consume(src, t):
        _softmax_update(src[...], keys(t), m_scr, acc_scr)

    def first_stage(qv, t):
        scores(qv, t, sl_scr)
        sm_scr[...] = _dot_nt(qv, kmeta)

    def prefetch():
        first_stage(qn_ref[...].reshape(rows, KCAT), jnp.minimum(((i + 1) * tq) // tk, n_tiles - 1))

    @pl.when(i == 0)
    def _():
        first_stage(q, 0)

    scores(q, 0, sa_scr)
    limit = ((i * tq + lax.broadcasted_iota(jnp.int32, (tq, tk), 0)) // CHUNK + 1) * CHUNK - n * tk
    bias = jnp.where(lax.broadcasted_iota(jnp.int32, (tq, tk), 1) < limit, 0.0, NEG)
    s_last = sl_scr[...] + jnp.concatenate([bias] * MLA_HEADS, axis=0)
    bias_meta = jnp.where(lax.broadcasted_iota(jnp.int32, (tq, LANES), 1) < N_META, 0.0, NEG)
    s_meta = sm_scr[...] + jnp.concatenate([bias_meta] * MLA_HEADS, axis=0)
    m0 = jnp.maximum(jnp.max(s_last, axis=1, keepdims=True), jnp.max(s_meta, axis=1, keepdims=True))
    m0 = jnp.broadcast_to(m0, (rows, LANES))
    p_last = jnp.exp2(s_last - _lanes(m0, tk // LANES)).astype(BF16)
    p_meta = jnp.exp2(s_meta - m0).astype(BF16)
    acc_scr[...] = _dot(p_last, keys(n)) + _dot(p_meta, kmeta)
    m_scr[...] = m0

    def body(p, carry):
        t0 = 2 * p
        scores(q, t0 + 1, sb_scr)
        consume(sa_scr, t0)
        scores(q, t0 + 2, sa_scr)
        consume(sb_scr, t0 + 1)
        return carry

    lax.fori_loop(0, jnp.maximum(n - 1, 0) // 2, body, 0)

    @pl.when(n == 0)
    def _():
        prefetch()

    @pl.when(n % 2 == 1)
    def _():
        prefetch()
        consume(sa_scr, n - 1)

    @pl.when((n >= 2) & (n % 2 == 0))
    def _():
        scores(q, n - 1, sb_scr)
        consume(sa_scr, n - 2)
        prefetch()
        consume(sb_scr, n - 1)

    _attn_finish(acc_scr[...], o_ref, tq)


def _attn_prompt(qcat, kmeta, kcat, batch, tq, tk):
    t_all = kcat.shape[0]
    seq = t_all // batch
    nq = seq // tq
    rows = MLA_HEADS * tq
    last_block = t_all // tq - 1
    return pl.pallas_call(
        functools.partial(_attn_prompt_body, tq=tq, tk=tk),
        grid=(batch, nq),
        in_specs=[pl.BlockSpec((MLA_HEADS, tq, KCAT), lambda b, i: (0, b * nq + i, 0)),
                  pl.BlockSpec((MLA_HEADS, tq, KCAT), lambda b, i: (0, jnp.minimum(b * nq + i + 1, last_block), 0)),
                  _const_spec(kmeta.shape),
                  pl.BlockSpec((seq, KCAT), lambda b, i: (b, 0))],
        out_specs=pl.BlockSpec((tq, MLA_HEADS * LANES), lambda b, i: (b * nq + i, 0)),
        out_shape=jax.ShapeDtypeStruct((t_all, MLA_HEADS * LANES), BF16),
        scratch_shapes=[pltpu.VMEM((rows, LANES), F32), pltpu.VMEM((rows, KCAT), F32),
                        pltpu.VMEM((rows, tk), F32), pltpu.VMEM((rows, LANES), F32),
                        pltpu.VMEM((rows, tk), F32), pltpu.VMEM((rows, tk), F32)],
        compiler_params=_params(2),
        name="attn_prompt",
    )(qcat, qcat, kmeta, kcat)


def _attn_sample_body(q_ref, kc_ref, kn_ref, o_ref, *, tq, ns):
    rows = MLA_HEADS * tq
    for s in range(ns):
        q = q_ref[:, s * tq:(s + 1) * tq, :].reshape(rows, KCAT)
        kc = kc_ref[s]
        kn = kn_ref[s * tq:(s + 1) * tq, :]
        s_c = _dot_nt(q, kc)
        s_n = _dot_nt(q, kn)
        m = jnp.maximum(jnp.max(s_c, axis=1, keepdims=True), jnp.max(s_n, axis=1, keepdims=True))
        acc = _dot(jnp.exp2(s_c - m).astype(BF16), kc) + _dot(jnp.exp2(s_n - m).astype(BF16), kn)
        _attn_finish(acc, o_ref, tq, s * tq)


def _attn_sample(qcat, kcache, kcat, tq, ns):
    batch = kcache.shape[0]
    t_all = kcat.shape[0]
    return pl.pallas_call(
        functools.partial(_attn_sample_body, tq=tq, ns=ns),
        grid=(batch // ns,),
        in_specs=[pl.BlockSpec((MLA_HEADS, ns * tq, KCAT), lambda b: (0, b, 0)),
                  pl.BlockSpec((ns,) + kcache.shape[1:], lambda b: (b, 0, 0)),
                  pl.BlockSpec((ns * tq, KCAT), lambda b: (b, 0))],
        out_specs=pl.BlockSpec((ns * tq, MLA_HEADS * LANES), lambda b: (b, 0)),
        out_shape=jax.ShapeDtypeStruct((t_all, MLA_HEADS * LANES), BF16),
        compiler_params=_params(1),
        name="attn_sample",
    )(qcat, kcache, kcat)


def _outproj_body(x_ref, olat_ref, ohg_ref, wuv_ref, wo_ref, fn_ref, wr_ref, br_ref,
                  h1_ref, hn_ref, gate_ref, *, n_split):
    tm = x_ref.shape[0] // n_split
    for r in range(n_split):
        rows = slice(r * tm, (r + 1) * tm)
        _outproj_rows(x_ref.at[rows], olat_ref.at[rows], ohg_ref.at[rows], wuv_ref, wo_ref, fn_ref, wr_ref,
                      br_ref, h1_ref.at[rows], hn_ref.at[rows], gate_ref.at[rows])


def _outproj_rows(x_ref, olat_ref, ohg_ref, wuv_ref, wo_ref, fn_ref, wr_ref, br_ref, h1_ref, hn_ref, gate_ref):
    o_mla = _dot(olat_ref[...], wuv_ref[...]).astype(BF16)
    mix = _dot(o_mla, wo_ref[:MLA_WIDTH, :]) + _dot(ohg_ref[...], wo_ref[MLA_WIDTH:, :])
    h1 = x_ref[...] + mix
    h1_ref[...] = h1
    hn = _rms(h1, fn_ref[...])
    hn_hi = hn.astype(BF16)
    hn_ref[...] = hn_hi
    hn_lo = (hn - hn_hi.astype(F32)).astype(BF16)
    r2 = _dot(jnp.concatenate([hn_hi, hn_lo], axis=1), wr_ref[...])
    logits = r2[:, :LANES] + r2[:, LANES:] + br_ref[...]
    lane = lax.broadcasted_iota(jnp.int32, logits.shape, 1)
    gl = jnp.where(lane < N_GROUPS, logits, NEG)
    gmax = jnp.max(gl, axis=1, keepdims=True)
    g_top = 1.0 / jnp.sum(jnp.exp(gl - gmax), axis=1, keepdims=True)
    g_idx = jnp.min(jnp.where(gl == gmax, lane, LANES), axis=1, keepdims=True)
    lo = GATE_LANE0 + EXPERTS_PER_GROUP * g_idx
    el = jnp.where((lane >= lo) & (lane < lo + EXPERTS_PER_GROUP), logits, NEG)
    e1 = jnp.max(el, axis=1, keepdims=True)
    i1 = jnp.min(jnp.where(el == e1, lane, LANES), axis=1, keepdims=True)
    el2 = jnp.where(lane == i1, NEG, el)
    e2 = jnp.max(el2, axis=1, keepdims=True)
    i2 = jnp.min(jnp.where(el2 == e2, lane, LANES), axis=1, keepdims=True)
    esum = jnp.sum(jnp.exp(el - e1), axis=1, keepdims=True)
    p1 = 1.0 / esum
    p2 = jnp.exp(e2 - e1) / esum
    w1 = p1 / (p1 + p2) * g_top
    w2 = p2 / (p1 + p2) * g_top
    gate_ref[...] = jnp.where(lane == i1, w1, jnp.where(lane == i2, w2, 0.0))


def _outproj(x2d, olat, ohg, wts, tm):
    t = x2d.shape[0]
    row = lambda w: pl.BlockSpec((tm, w), lambda i: (i, 0))
    return pl.pallas_call(
        functools.partial(_outproj_body, n_split=max(1, tm // 256)),
        grid=(t // tm,),
        in_specs=[row(D_MODEL), row(MLA_HEADS * LANES), row(HG_WIDTH)] + [_const_spec(w.shape) for w in wts],
        out_specs=[row(D_MODEL), row(D_MODEL), row(LANES)],
        out_shape=[jax.ShapeDtypeStruct((t, D_MODEL), F32), jax.ShapeDtypeStruct((t, D_MODEL), BF16),
                   jax.ShapeDtypeStruct((t, LANES), F32)],
        compiler_params=_params(1),
        name="outproj",
    )(x2d, olat, ohg, *wts)


def _moe_body(hn_ref, h1_ref, gate_ref, w1_ref, w3_ref, w2_ref, fin_ref, y_ref):
    t = hn_ref[...]
    gates = gate_ref[...]
    acc = jnp.zeros(h1_ref.shape, F32)
    for grp in range(N_GROUPS):
        parts = []
        for j in range(EXPERTS_PER_GROUP):
            e = grp * EXPERTS_PER_GROUP + j
            a = _dot(t, w1_ref[e])
            c = _dot(t, w3_ref[e])
            lane = GATE_LANE0 + e
            parts.append((a * jax.nn.sigmoid(a) * c * gates[:, lane:lane + 1]).astype(BF16))
        acc = acc + _dot(jnp.concatenate(parts, axis=-1), w2_ref[grp])
    y_ref[...] = _rms(h1_ref[...] + acc, fin_ref[...])


def _moe(hn, h1, gates, w1g, w3g, w2g, fin, tm):
    t = hn.shape[0]
    row = lambda w: pl.BlockSpec((tm, w), lambda i: (i, 0))
    resident = lambda a: pl.BlockSpec(a.shape, lambda i: (0,) * a.ndim, pipeline_mode=pl.Buffered(1))
    return pl.pallas_call(
        _moe_body,
        grid=(t // tm,),
        in_specs=[row(D_MODEL), row(D_MODEL), row(LANES), resident(w1g), resident(w3g), resident(w2g),
                  _const_spec(fin.shape)],
        out_specs=row(D_MODEL),
        out_shape=jax.ShapeDtypeStruct((t, D_MODEL), F32),
        compiler_params=_params(1),
        name="moe",
    )(hn, h1, gates, w1g, w3g, w2g, fin)


def _rope_tables(pos):
    half = ROPE_DIM // 2
    inv = ROPE_THETA ** (-jnp.arange(half, dtype=F32) / half)
    ang = pos.astype(F32)[:, None] * inv[None, :]
    cos, sin = jnp.cos(ang), jnp.sin(ang)
    n = pos.shape[0]
    z = lambda w: jnp.zeros((n, w), F32)
    cosk = jnp.concatenate([cos, cos, z(LANES - ROPE_DIM)], axis=1)
    sink = jnp.concatenate([-sin, sin, z(LANES - ROPE_DIM)], axis=1)
    cosq = jnp.concatenate([jnp.ones((n, QK_NOPE_DIM), F32), cos, cos, z(ROPE_DIM)], axis=1)
    sinq = jnp.concatenate([z(QK_NOPE_DIM), -sin, sin, z(ROPE_DIM)], axis=1)
    return cosk, sink, cosq, sinq


def kernel(x_prompt, x_sample, cache_ckv, cache_krope, state_hgrn, meta_tokens, attn_norm, w_in, q_norm, kv_norm, w_uq, w_uk, w_uv, hg_lb_logits, hg_gnorm, w_o, ffn_norm, w_router_group, b_router_group, w_router_expert, b_router_expert, w1, w3, w2, final_norm):
    bp, seq, _ = x_prompt.shape
    bs, dseq, _ = x_sample.shape
    half = ROPE_DIM // 2
    swap = np.concatenate([np.arange(half, ROPE_DIM), np.arange(half)])

    lb = jax.nn.softmax(hg_lb_logits.astype(F32), axis=0)[0][None, :]
    wi = w_in[0]
    o_kv, o_kr, o_hg = Q_LORA_RANK, Q_LORA_RANK + KV_LORA_RANK, Q_LORA_RANK + KV_LORA_RANK + ROPE_DIM
    w_kr = wi[:, o_kr:o_hg]
    win_ext = jnp.concatenate([wi[:, :o_kr], w_kr, w_kr[:, swap], jnp.zeros((D_MODEL, LANES - 2 * ROPE_DIM), F32),
                               wi[:, o_hg:]], axis=1).astype(BF16)
    wq = w_uq[0]
    wq_rope = wq[:, :, QK_NOPE_DIM:]
    wuq_ext = jnp.concatenate([wq, wq_rope[:, :, swap]], axis=2).reshape(Q_LORA_RANK, MLA_HEADS * LANES).astype(BF16)
    eye = jnp.eye(ROPE_DIM, dtype=F32)
    w2q = jnp.zeros((MLA_HEADS, LANES, KCAT), F32)
    w2q = w2q.at[:, :QK_NOPE_DIM, :KV_LORA_RANK].set(jnp.transpose(w_uk[0], (1, 2, 0)))
    w2q = w2q.at[:, QK_NOPE_DIM:QK_NOPE_DIM + ROPE_DIM, KV_LORA_RANK:KV_LORA_RANK + ROPE_DIM].set(eye[None])
    w2q = w2q.astype(BF16)
    wuv_bd = jnp.zeros((MLA_HEADS, KV_LORA_RANK, MLA_HEADS, V_HEAD_DIM), F32)
    for h in range(MLA_HEADS):
        wuv_bd = wuv_bd.at[h, :, h, :].set(w_uv[0][:, h, :])
    wuv_bd = wuv_bd.reshape(MLA_HEADS * KV_LORA_RANK, MLA_WIDTH).astype(BF16)
    wo = w_o[0].astype(BF16)
    wr = jnp.concatenate([w_router_group[0], w_router_expert[0],
                          jnp.zeros((D_MODEL, LANES - N_GROUPS - N_EXPERTS), F32)], axis=1)
    wr_hi = wr.astype(BF16)
    wr_lo = (wr - wr_hi.astype(F32)).astype(BF16)
    wr2 = jnp.concatenate([jnp.concatenate([wr_hi, wr_lo], axis=1),
                           jnp.concatenate([wr_hi, jnp.zeros_like(wr_lo)], axis=1)], axis=0)
    br = jnp.concatenate([b_router_group[0], b_router_expert[0],
                          jnp.zeros((LANES - N_GROUPS - N_EXPERTS,), F32)])[None, :].astype(F32)
    w1g = w1[0].astype(BF16)
    w3g = w3[0].astype(BF16)
    w2g = w2[0].astype(BF16).reshape(N_GROUPS, EXPERTS_PER_GROUP * EXPERT_FF, D_MODEL)
    in_wts = (attn_norm[0][None, :], win_ext, q_norm[0][None, :], kv_norm[0][None, :], wuq_ext, w2q)
    out_wts = (wuv_bd, wo, ffn_norm[0][None, :], wr2, br)
    gn = hg_gnorm[0][None, :]
    fin = final_norm[None, :]

    tm_s = 256
    tab_meta = _rope_tables(jnp.arange(N_META))
    tab_p = _rope_tables(N_META + jnp.arange(seq))
    tab_s = tuple(jnp.tile(a, (tm_s // dseq, 1)) for a in _rope_tables(PAST_LEN + jnp.arange(dseq)))
    ckv_m, kr_m, kcat_m, _, hz_m = _inproj(meta_tokens, tab_meta, in_wts, N_META)
    xp = x_prompt.reshape(bp * seq, D_MODEL)
    xs = x_sample.reshape(bs * dseq, D_MODEL)
    ckv_p, kr_p, kcat_p, qcat_p, hz_p = _inproj(xp, tab_p, in_wts, 512)
    ckv_s, kr_s, kcat_s, qcat_s, hz_s = _inproj(xs, tab_s, in_wts, tm_s)

    zero_state = jnp.zeros((1, HG_HEADS, HG_KEY_DIM, HG_VAL_DIM), F32)
    _, st_m = _hgrn(hz_m, lb, gn, zero_state, 1, N_META)
    ohg_p, st_p = _hgrn(hz_p, lb, gn, st_m, bp, 128)
    ohg_s, st_s = _hgrn(hz_s, lb, gn, state_hgrn[0].astype(F32), bs, dseq)

    kmeta = jnp.concatenate([kcat_m, jnp.zeros((LANES - N_META, KCAT), BF16)], axis=0)
    olat_p = _attn_prompt(qcat_p, kmeta, kcat_p, bp, 128, 512)
    n_c = cache_ckv.shape[2]
    kcache = jnp.concatenate([cache_ckv[0], cache_krope[0], jnp.ones((bs, n_c, 1), F32),
                              jnp.zeros((bs, n_c, KCAT - ONE_COL - 1), F32)], axis=-1).astype(BF16)
    olat_s = _attn_sample(qcat_s, kcache, kcat_s, dseq, 2)

    h1_p, hn_p, gate_p = _outproj(xp, olat_p, ohg_p, out_wts, 512)
    h1_s, hn_s, gate_s = _outproj(xs, olat_s, ohg_s, out_wts, 256)
    y_p = _moe(hn_p, h1_p, gate_p, w1g, w3g, w2g, fin, 256)
    y_s = _moe(hn_s, h1_s, gate_s, w1g, w3g, w2g, fin, 256)

    with_meta = lambda m, a, w: jnp.concatenate(
        [jnp.broadcast_to(m[None], (bp, N_META, w)), a.reshape(bp, seq, w)], axis=1)[None]
    return (y_p.reshape(bp, seq, D_MODEL), y_s.reshape(bs, dseq, D_MODEL),
            with_meta(ckv_m, ckv_p, KV_LORA_RANK), with_meta(kr_m, kr_p, ROPE_DIM), st_p[None],
            ckv_s.reshape(1, bs, dseq, KV_LORA_RANK), kr_s.reshape(1, bs, dseq, ROPE_DIM), st_s[None])
```

```python
import functools

import numpy as np
import jax
import jax.numpy as jnp
from jax import lax
from jax.experimental import pallas as pl
from jax.experimental.pallas import tpu as pltpu

F32 = jnp.float32
BF16 = jnp.bfloat16

D_MODEL = 1024
PAST_LEN = 2048
CHUNK = 64
N_META = 16
EPS = 1e-6
MLA_HEADS = 8
Q_LORA_RANK = 256
KV_LORA_RANK = 128
QK_NOPE_DIM = 64
ROPE_DIM = 32
V_HEAD_DIM = 64
ROPE_THETA = 10000.0
MLA_WIDTH = MLA_HEADS * V_HEAD_DIM
SM_SCALE = (QK_NOPE_DIM + ROPE_DIM) ** -0.5
LOG2E = 1.4426950408889634
Q_SCALE = SM_SCALE * LOG2E
HG_HEADS = 4
HG_KEY_DIM = 128
HG_VAL_DIM = 128
HG_WIDTH = HG_HEADS * HG_VAL_DIM
HG_BLOCK = 16
N_GROUPS = 4
EXPERTS_PER_GROUP = 4
N_EXPERTS = N_GROUPS * EXPERTS_PER_GROUP
EXPERT_FF = 256

LANES = 128
KCAT = 2 * LANES
ONE_COL = LANES + ROPE_DIM
NEG = -1e30
VMEM_LIMIT = 56 * 1024 * 1024
GATE_LANE0 = N_GROUPS


def _rms(x, w):
    return x * lax.rsqrt(jnp.mean(x * x, axis=-1, keepdims=True) + EPS) * w


def _dot(a, b):
    return jnp.dot(a, b, preferred_element_type=F32)


def _dot_nt(a, b):
    return lax.dot_general(a, b, (((1,), (1,)), ((), ())), preferred_element_type=F32)


def _dot_tn(a, b):
    return lax.dot_general(a, b, (((0,), (0,)), ((), ())), preferred_element_type=F32)


def _split3(x):
    x1 = x.astype(BF16)
    r = x - x1.astype(F32)
    x2 = r.astype(BF16)
    x3 = (r - x2.astype(F32)).astype(BF16)
    return x1, x2, x3


def _params(n_axes):
    return pltpu.CompilerParams(dimension_semantics=("arbitrary",) * n_axes, vmem_limit_bytes=VMEM_LIMIT)


def _const_spec(shape):
    nd = len(shape)
    return pl.BlockSpec(shape, lambda *_: (0,) * nd)


def _inproj_body(x_ref, cosk_ref, sink_ref, cosq_ref, sinq_ref, an_ref, win_ref, qn_ref, kvn_ref, wuq_ref, w2_ref,
                 ckv_ref, krope_ref, kcat_ref, qcat_ref, hz_ref):
    x = x_ref[...]
    hn = _rms(x, an_ref[...]).astype(BF16)
    z = _dot(hn, win_ref[...])
    hz_ref[...] = z[:, 4 * LANES:]
    ckv = _rms(z[:, 2 * LANES:3 * LANES], kvn_ref[...])
    ckv_ref[...] = ckv
    kr = z[:, 3 * LANES:4 * LANES]
    rot = kr * cosk_ref[...] + pltpu.roll(kr, LANES - ROPE_DIM, 1) * sink_ref[...]
    krope_ref[...] = rot[:, :ROPE_DIM]
    lane = lax.broadcasted_iota(jnp.int32, rot.shape, 1)
    kcat_ref[...] = jnp.concatenate([ckv, jnp.where(lane == ROPE_DIM, 1.0, rot)], axis=-1).astype(BF16)
    cqn = _rms(z[:, :Q_LORA_RANK], qn_ref[...]).astype(BF16)
    qz = _dot(cqn, wuq_ref[...])
    cosq = cosq_ref[...]
    sinq = sinq_ref[...]
    for h in range(MLA_HEADS):
        blk = qz[:, h * LANES:(h + 1) * LANES]
        r = blk * cosq + pltpu.roll(blk, LANES - ROPE_DIM, 1) * sinq
        qcat_ref[h] = (_dot(r.astype(BF16), w2_ref[h]) * Q_SCALE).astype(BF16)


def _inproj(x2d, tabs, wts, tm):
    t = x2d.shape[0]
    period = tabs[0].shape[0] // tm
    grid = (t // tm,)
    row = lambda w: pl.BlockSpec((tm, w), lambda i: (i, 0))
    tab = pl.BlockSpec((tm, LANES), lambda i: (i % period, 0))
    an, win, qn, kvn, wuq, w2 = wts
    return pl.pallas_call(
        _inproj_body,
        grid=grid,
        in_specs=[row(D_MODEL), tab, tab, tab, tab, _const_spec(an.shape), _const_spec(win.shape),
                  _const_spec(qn.shape), _const_spec(kvn.shape), _const_spec(wuq.shape), _const_spec(w2.shape)],
        out_specs=[row(KV_LORA_RANK), row(ROPE_DIM), row(KCAT),
                   pl.BlockSpec((MLA_HEADS, tm, KCAT), lambda i: (0, i, 0)), row(4 * HG_WIDTH)],
        out_shape=[jax.ShapeDtypeStruct((t, KV_LORA_RANK), F32), jax.ShapeDtypeStruct((t, ROPE_DIM), F32),
                   jax.ShapeDtypeStruct((t, KCAT), BF16), jax.ShapeDtypeStruct((MLA_HEADS, t, KCAT), BF16),
                   jax.ShapeDtypeStruct((t, 4 * HG_WIDTH), F32)],
        compiler_params=_params(1),
        name="inproj",
    )(x2d, *tabs, an, win, qn, kvn, wuq, w2)


def _hgrn_body(hq_ref, hf_ref, hi_ref, hg_ref, lb_ref, gn_ref, s0_ref, tri_ref, lv_ref, cst_ref,
               o_ref, sout_ref, st_ref, *, tt, n_sub):
    step = pl.program_id(1)
    n_levels = tt.bit_length() - 1

    @pl.when(step == 0)
    def _():
        for h in range(HG_HEADS):
            st_ref[h] = s0_ref[0, h].T

    lb = lb_ref[...]
    gn = gn_ref[...]
    tri = tri_ref[...]
    lv = lv_ref[...]
    for sub in range(n_sub):
        rows = slice(sub * tt, (sub + 1) * tt)
        hq = hq_ref[rows, :]
        q = hq * jax.nn.sigmoid(hq) * (HG_KEY_DIM ** -0.5)
        f = lb + (1.0 - lb) * jax.nn.sigmoid(hf_ref[rows, :])
        g = jnp.log(f) * LOG2E
        kk = 1.0 - f
        v = hi_ref[rows, :]
        g1, g2, g3 = _split3(g)
        bc = _dot(tri, g1) + _dot(tri, g2) + _dot(tri, g3)

        bl = bc[tt - 1:tt, :]
        qs = (q * jnp.exp2(bc)).astype(BF16)
        kf = (kk * jnp.exp2(bl - bc)).astype(BF16)
        vb = v.astype(BF16)
        dl = jnp.exp2(bl)
        hg = hg_ref[rows, :]
        for h in range(HG_HEADS):
            cols = slice(h * LANES, (h + 1) * LANES)
            qh, kh, gh, bch = q[:, cols], kk[:, cols], g[:, cols], bc[:, cols]
            pairs = [(qh, kh)]
            pairs.append((qh * jnp.exp2(gh + cst_ref[0]), kh * cst_ref[1]))
            if n_levels > 1:
                g_prev = pltpu.roll(gh, 1, 0)
                g_next = pltpu.roll(gh, tt - 1, 0)
                pairs.append((qh * jnp.exp2(gh + g_prev * cst_ref[2] + cst_ref[3]),
                              kh * jnp.exp2(g_next * cst_ref[4] + cst_ref[5])))
            for lvl in range(2, n_levels):
                grp = 2 << lvl
                mid = bch.reshape(tt // grp, grp, LANES)[:, grp // 2 - 1:grp // 2, :]
                d = bch - jnp.broadcast_to(mid, (tt // grp, grp, LANES)).reshape(tt, LANES)
                pairs.append((qh * jnp.exp2(d + cst_ref[2 + 2 * lvl]), kh * jnp.exp2(cst_ref[3 + 2 * lvl] - d)))
            att = jnp.zeros((tt, tt), F32)
            for idx, (qa, ka) in enumerate(pairs):
                att = jnp.where(lv == idx - 1, _dot_nt(qa.astype(BF16), ka.astype(BF16)), att)
            st = st_ref[h]
            o = _dot(att.astype(BF16), vb[:, cols]) + _dot_nt(qs[:, cols], st.astype(BF16))
            st_ref[h] = dl[:, cols] * st + _dot_tn(vb[:, cols], kf[:, cols])
            gate = hg[:, cols]
            o_ref[rows, cols] = (_rms(o, gn) * (gate * jax.nn.sigmoid(gate))).astype(BF16)

    @pl.when(step == pl.num_programs(1) - 1)
    def _():
        for h in range(HG_HEADS):
            sout_ref[0, h] = st_ref[h].T


def _hgrn(hz, lb, gn, s0, batch, tt, n_sub=1):
    t_all = hz.shape[0]
    blk = tt * n_sub
    steps = t_all // batch // blk
    idx = np.arange(tt)
    tri = jnp.asarray(idx[None, :] <= idx[:, None], BF16)
    diff = idx[:, None] ^ idx[None, :]
    level = np.where(diff > 0, np.floor(np.log2(np.maximum(diff, 1))), -1).astype(np.int32)
    lv = jnp.asarray(np.where(idx[None, :] <= idx[:, None], level, tt), jnp.int32)
    rows = np.broadcast_to(idx[:, None], (tt, LANES))
    bias = lambda keep: np.where(keep, 0.0, NEG)
    one = lambda keep: np.where(keep, 1.0, 0.0)
    cst = [bias(rows % 2 == 1), one(rows % 2 == 0),
           one(rows % 4 == 3), bias(rows % 4 >= 2), one(rows % 4 == 0), bias(rows % 4 < 2)]
    for lvl in range(2, tt.bit_length() - 1):
        upper = (rows >> lvl) % 2 == 1
        cst += [bias(upper), bias(~upper)]
    cst = jnp.asarray(np.stack(cst), F32)
    col = lambda c: pl.BlockSpec((blk, HG_WIDTH), lambda b, s: (b * steps + s, c))
    shared_s0 = s0.shape[0] == 1
    state = lambda shared: pl.BlockSpec((1, HG_HEADS, HG_KEY_DIM, HG_VAL_DIM),
                                        (lambda b, s: (0, 0, 0, 0)) if shared else (lambda b, s: (b, 0, 0, 0)))
    return pl.pallas_call(
        functools.partial(_hgrn_body, tt=tt, n_sub=n_sub),
        grid=(batch, steps),
        in_specs=[col(0), col(1), col(2), col(3), _const_spec(lb.shape), _const_spec(gn.shape), state(shared_s0),
                  _const_spec(tri.shape), _const_spec(lv.shape), _const_spec(cst.shape)],
        out_specs=[pl.BlockSpec((blk, HG_WIDTH), lambda b, s: (b * steps + s, 0)), state(False)],
        out_shape=[jax.ShapeDtypeStruct((t_all, HG_WIDTH), BF16),
                   jax.ShapeDtypeStruct((batch, HG_HEADS, HG_KEY_DIM, HG_VAL_DIM), F32)],
        scratch_shapes=[pltpu.VMEM((HG_HEADS, HG_VAL_DIM, HG_KEY_DIM), F32)],
        compiler_params=_params(2),
        name="hgrn",
    )(hz, hz, hz, hz, lb, gn, s0, tri, lv, cst)


def _lanes(x, n):
    return jnp.concatenate([x] * n, axis=1)


def _softmax_update(s, kt, m_scr, acc_scr):
    m_prev = m_scr[...]
    m_new = jnp.maximum(m_prev, jnp.max(s, axis=1, keepdims=True))
    alpha = jnp.exp2(m_prev - m_new)
    p = jnp.exp2(s - _lanes(m_new, s.shape[1] // LANES))
    acc_scr[...] = _lanes(alpha, KCAT // LANES) * acc_scr[...] + _dot(p.astype(BF16), kt)
    m_scr[...] = m_new


def _attn_finish(acc, o_ref, tq, row0=0):
    o = acc[:, :KV_LORA_RANK] / acc[:, ONE_COL:ONE_COL + 1]
    for h in range(MLA_HEADS):
        o_ref[row0:row0 + tq, h * LANES:(h + 1) * LANES] = o[h * tq:(h + 1) * tq].astype(BF16)


def _attn_prompt_body(q_ref, qn_ref, kmeta_ref, k_ref, o_ref, m_scr, acc_scr, sl_scr, sm_scr, sa_scr, sb_scr,
                      *, tq, tk):
    i = pl.program_id(1)
    rows = MLA_HEADS * tq
    n_tiles = k_ref.shape[0] // tk
    q = q_ref[...].reshape(rows, KCAT)
    kmeta = kmeta_ref[...]
    n = (i * tq) // tk

    def keys(t):
        return k_ref[pl.ds(pl.multiple_of(t * tk, tk), tk), :]

    def scores(qv, t, dst):
        dst[...] = _dot_nt(qv, keys(t))

    def consume(src, t):
        _softmax_update(src[...], keys(t), m_scr, acc_scr)

    def first_stage(qv, t):
        scores(qv, t, sl_scr)
        sm_scr[...] = _dot_nt(qv, kmeta)

    def prefetch():
        first_stage(qn_ref[...].reshape(rows, KCAT), jnp.minimum(((i + 1) * tq) // tk, n_tiles - 1))

    @pl.when(i == 0)
    def _():
        first_stage(q, 0)

    scores(q, 0, sa_scr)
    limit = ((i * tq + lax.broadcasted_iota(jnp.int32, (tq, tk), 0)) // CHUNK + 1) * CHUNK - n * tk
    bias = jnp.where(lax.broadcasted_iota(jnp.int32, (tq, tk), 1) < limit, 0.0, NEG)
    s_last = sl_scr[...] + jnp.concatenate([bias] * MLA_HEADS, axis=0)
    bias_meta = jnp.where(lax.broadcasted_iota(jnp.int32, (tq, LANES), 1) < N_META, 0.0, NEG)
    s_meta = sm_scr[...] + jnp.concatenate([bias_meta] * MLA_HEADS, axis=0)
    m0 = jnp.maximum(jnp.max(s_last, axis=1, keepdims=True), jnp.max(s_meta, axis=1, keepdims=True))
    m0 = jnp.broadcast_to(m0, (rows, LANES))
    p_last = jnp.exp2(s_last - _lanes(m0, tk // LANES)).astype(BF16)
    p_meta = jnp.exp2(s_meta - m0).astype(BF16)
    acc_scr[...] = _dot(p_last, keys(n)) + _dot(p_meta, kmeta)
    m_scr[...] = m0

    def body(p, carry):
        t0 = 2 * p
        scores(q, t0 + 1, sb_scr)
        consume(sa_scr, t0)
        scores(q, t0 + 2, sa_scr)
        consume(sb_scr, t0 + 1)
        return carry

    lax.fori_loop(0, jnp.maximum(n - 1, 0) // 2, body, 0)

    @pl.when(n == 0)
    def _():
        prefetch()

    @pl.when(n % 2 == 1)
    def _():
        prefetch()
        consume(sa_scr, n - 1)

    @pl.when((n >= 2) & (n % 2 == 0))
    def _():
        scores(q, n - 1, sb_scr)
        consume(sa_scr, n - 2)
        prefetch()
        consume(sb_scr, n - 1)

    _attn_finish(acc_scr[...], o_ref, tq)


def _attn_prompt(qcat, kmeta, kcat, batch, tq, tk):
    t_all = kcat.shape[0]
    seq = t_all // batch
    nq = seq // tq
    rows = MLA_HEADS * tq
    last_block = t_all // tq - 1
    return pl.pallas_call(
        functools.partial(_attn_prompt_body, tq=tq, tk=tk),
        grid=(batch, nq),
        in_specs=[pl.BlockSpec((MLA_HEADS, tq, KCAT), lambda b, i: (0, b * nq + i, 0)),
                  pl.BlockSpec((MLA_HEADS, tq, KCAT), lambda b, i: (0, jnp.minimum(b * nq + i + 1, last_block), 0)),
                  _const_spec(kmeta.shape),
                  pl.BlockSpec((seq, KCAT), lambda b, i: (b, 0))],
        out_specs=pl.BlockSpec((tq, MLA_HEADS * LANES), lambda b, i: (b * nq + i, 0)),
        out_shape=jax.ShapeDtypeStruct((t_all, MLA_HEADS * LANES), BF16),
        scratch_shapes=[pltpu.VMEM((rows, LANES), F32), pltpu.VMEM((rows, KCAT), F32),
                        pltpu.VMEM((rows, tk), F32), pltpu.VMEM((rows, LANES), F32),
                        pltpu.VMEM((rows, tk), F32), pltpu.VMEM((rows, tk), F32)],
        compiler_params=_params(2),
        name="attn_prompt",
    )(qcat, qcat, kmeta, kcat)


def _attn_sample_body(q_ref, kc_ref, kn_ref, o_ref, *, tq, ns):
    rows = MLA_HEADS * tq
    for s in range(ns):
        q = q_ref[:, s * tq:(s + 1) * tq, :].reshape(rows, KCAT)
        kc = kc_ref[s]
        kn = kn_ref[s * tq:(s + 1) * tq, :]
        s_c = _dot_nt(q, kc)
        s_n = _dot_nt(q, kn)
        m = jnp.maximum(jnp.max(s_c, axis=1, keepdims=True), jnp.max(s_n, axis=1, keepdims=True))
        acc = _dot(jnp.exp2(s_c - m).astype(BF16), kc) + _dot(jnp.exp2(s_n - m).astype(BF16), kn)
        _attn_finish(acc, o_ref, tq, s * tq)


def _attn_sample(qcat, kcache, kcat, tq, ns):
    batch = kcache.shape[0]
    t_all = kcat.shape[0]
    return pl.pallas_call(
        functools.partial(_attn_sample_body, tq=tq, ns=ns),
        grid=(batch // ns,),
        in_specs=[pl.BlockSpec((MLA_HEADS, ns * tq, KCAT), lambda b: (0, b, 0)),
                  pl.BlockSpec((ns,) + kcache.shape[1:], lambda b: (b, 0, 0)),
                  pl.BlockSpec((ns * tq, KCAT), lambda b: (b, 0))],
        out_specs=pl.BlockSpec((ns * tq, MLA_HEADS * LANES), lambda b: (b, 0)),
        out_shape=jax.ShapeDtypeStruct((t_all, MLA_HEADS * LANES), BF16),
        compiler_params=_params(1),
        name="attn_sample",
    )(qcat, kcache, kcat)


def _outproj_body(x_ref, olat_ref, ohg_ref, wuv_ref, wo_ref, fn_ref, wr_ref, br_ref,
                  h1_ref, hn_ref, gate_ref, *, n_split):
    tm = x_ref.shape[0] // n_split
    for r in range(n_split):
        rows = slice(r * tm, (r + 1) * tm)
        _outproj_rows(x_ref.at[rows], olat_ref.at[rows], ohg_ref.at[rows], wuv_ref, wo_ref, fn_ref, wr_ref,
                      br_ref, h1_ref.at[rows], hn_ref.at[rows], gate_ref.at[rows])


def _outproj_rows(x_ref, olat_ref, ohg_ref, wuv_ref, wo_ref, fn_ref, wr_ref, br_ref, h1_ref, hn_ref, gate_ref):
    o_mla = _dot(olat_ref[...], wuv_ref[...]).astype(BF16)
    mix = _dot(o_mla, wo_ref[:MLA_WIDTH, :]) + _dot(ohg_ref[...], wo_ref[MLA_WIDTH:, :])
    h1 = x_ref[...] + mix
    h1_ref[...] = h1
    hn = _rms(h1, fn_ref[...])
    hn_hi = hn.astype(BF16)
    hn_ref[...] = hn_hi
    hn_lo = (hn - hn_hi.astype(F32)).astype(BF16)
    r2 = _dot(jnp.concatenate([hn_hi, hn_lo], axis=1), wr_ref[...])
    logits = r2[:, :LANES] + r2[:, LANES:] + br_ref[...]
    lane = lax.broadcasted_iota(jnp.int32, logits.shape, 1)
    gl = jnp.where(lane < N_GROUPS, logits, NEG)
    gmax = jnp.max(gl, axis=1, keepdims=True)
    g_top = 1.0 / jnp.sum(jnp.exp(gl - gmax), axis=1, keepdims=True)
    g_idx = jnp.min(jnp.where(gl == gmax, lane, LANES), axis=1, keepdims=True)
    lo = GATE_LANE0 + EXPERTS_PER_GROUP * g_idx
    el = jnp.where((lane >= lo) & (lane < lo + EXPERTS_PER_GROUP), logits, NEG)
    e1 = jnp.max(el, axis=1, keepdims=True)
    i1 = jnp.min(jnp.where(el == e1, lane, LANES), axis=1, keepdims=True)
    el2 = jnp.where(lane == i1, NEG, el)
    e2 = jnp.max(el2, axis=1, keepdims=True)
    i2 = jnp.min(jnp.where(el2 == e2, lane, LANES), axis=1, keepdims=True)
    esum = jnp.sum(jnp.exp(el - e1), axis=1, keepdims=True)
    p1 = 1.0 / esum
    p2 = jnp.exp(e2 - e1) / esum
    w1 = p1 / (p1 + p2) * g_top
    w2 = p2 / (p1 + p2) * g_top
    gate_ref[...] = jnp.where(lane == i1, w1, jnp.where(lane == i2, w2, 0.0))


def _outproj(x2d, olat, ohg, wts, tm):
    t = x2d.shape[0]
    row = lambda w: pl.BlockSpec((tm, w), lambda i: (i, 0))
    return pl.pallas_call(
        functools.partial(_outproj_body, n_split=max(1, tm // 256)),
        grid=(t // tm,),
        in_specs=[row(D_MODEL), row(MLA_HEADS * LANES), row(HG_WIDTH)] + [_const_spec(w.shape) for w in wts],
        out_specs=[row(D_MODEL), row(D_MODEL), row(LANES)],
        out_shape=[jax.ShapeDtypeStruct((t, D_MODEL), F32), jax.ShapeDtypeStruct((t, D_MODEL), BF16),
                   jax.ShapeDtypeStruct((t, LANES), F32)],
        compiler_params=_params(1),
        name="outproj",
    )(x2d, olat, ohg, *wts)


def _moe_body(hn_ref, h1_ref, gate_ref, w1_ref, w3_ref, w2_ref, fin_ref, y_ref):
    t = hn_ref[...]
    gates = gate_ref[...]
    acc = jnp.zeros(h1_ref.shape, F32)
    for grp in range(N_GROUPS):
        parts = []
        for j in range(EXPERTS_PER_GROUP):
            e = grp * EXPERTS_PER_GROUP + j
            a = _dot(t, w1_ref[e])
            c = _dot(t, w3_ref[e])
            lane = GATE_LANE0 + e
            parts.append((a * jax.nn.sigmoid(a) * c * gates[:, lane:lane + 1]).astype(BF16))
        acc = acc + _dot(jnp.concatenate(parts, axis=-1), w2_ref[grp])
    y_ref[...] = _rms(h1_ref[...] + acc, fin_ref[...])


def _moe(hn, h1, gates, w1g, w3g, w2g, fin, tm):
    t = hn.shape[0]
    row = lambda w: pl.BlockSpec((tm, w), lambda i: (i, 0))
    resident = lambda a: pl.BlockSpec(a.shape, lambda i: (0,) * a.ndim, pipeline_mode=pl.Buffered(1))
    return pl.pallas_call(
        _moe_body,
        grid=(t // tm,),
        in_specs=[row(D_MODEL), row(D_MODEL), row(LANES), resident(w1g), resident(w3g), resident(w2g),
                  _const_spec(fin.shape)],
        out_specs=row(D_MODEL),
        out_shape=jax.ShapeDtypeStruct((t, D_MODEL), F32),
        compiler_params=_params(1),
        name="moe",
    )(hn, h1, gates, w1g, w3g, w2g, fin)


def _rope_tables(pos):
    half = ROPE_DIM // 2
    inv = ROPE_THETA ** (-jnp.arange(half, dtype=F32) / half)
    ang = pos.astype(F32)[:, None] * inv[None, :]
    cos, sin = jnp.cos(ang), jnp.sin(ang)
    n = pos.shape[0]
    z = lambda w: jnp.zeros((n, w), F32)
    cosk = jnp.concatenate([cos, cos, z(LANES - ROPE_DIM)], axis=1)
    sink = jnp.concatenate([-sin, sin, z(LANES - ROPE_DIM)], axis=1)
    cosq = jnp.concatenate([jnp.ones((n, QK_NOPE_DIM), F32), cos, cos, z(ROPE_DIM)], axis=1)
    sinq = jnp.concatenate([z(QK_NOPE_DIM), -sin, sin, z(ROPE_DIM)], axis=1)
    return cosk, sink, cosq, sinq


def kernel(x_prompt, x_sample, cache_ckv, cache_krope, state_hgrn, meta_tokens, attn_norm, w_in, q_norm, kv_norm, w_uq, w_uk, w_uv, hg_lb_logits, hg_gnorm, w_o, ffn_norm, w_router_group, b_router_group, w_router_expert, b_router_expert, w1, w3, w2, final_norm):
    bp, seq, _ = x_prompt.shape
    bs, dseq, _ = x_sample.shape
    half = ROPE_DIM // 2
    swap = np.concatenate([np.arange(half, ROPE_DIM), np.arange(half)])

    lb = jax.nn.softmax(hg_lb_logits.astype(F32), axis=0)[0][None, :]
    wi = w_in[0]
    o_kv, o_kr, o_hg = Q_LORA_RANK, Q_LORA_RANK + KV_LORA_RANK, Q_LORA_RANK + KV_LORA_RANK + ROPE_DIM
    w_kr = wi[:, o_kr:o_hg]
    win_ext = jnp.concatenate([wi[:, :o_kr], w_kr, w_kr[:, swap], jnp.zeros((D_MODEL, LANES - 2 * ROPE_DIM), F32),
                               wi[:, o_hg:]], axis=1).astype(BF16)
    wq = w_uq[0]
    wq_rope = wq[:, :, QK_NOPE_DIM:]
    wuq_ext = jnp.concatenate([wq, wq_rope[:, :, swap]], axis=2).reshape(Q_LORA_RANK, MLA_HEADS * LANES).astype(BF16)
    eye = jnp.eye(ROPE_DIM, dtype=F32)
    w2q = jnp.zeros((MLA_HEADS, LANES, KCAT), F32)
    w2q = w2q.at[:, :QK_NOPE_DIM, :KV_LORA_RANK].set(jnp.transpose(w_uk[0], (1, 2, 0)))
    w2q = w2q.at[:, QK_NOPE_DIM:QK_NOPE_DIM + ROPE_DIM, KV_LORA_RANK:KV_LORA_RANK + ROPE_DIM].set(eye[None])
    w2q = w2q.astype(BF16)
    wuv_bd = jnp.zeros((MLA_HEADS, KV_LORA_RANK, MLA_HEADS, V_HEAD_DIM), F32)
    for h in range(MLA_HEADS):
        wuv_bd = wuv_bd.at[h, :, h, :].set(w_uv[0][:, h, :])
    wuv_bd = wuv_bd.reshape(MLA_HEADS * KV_LORA_RANK, MLA_WIDTH).astype(BF16)
    wo = w_o[0].astype(BF16)
    wr = jnp.concatenate([w_router_group[0], w_router_expert[0],
                          jnp.zeros((D_MODEL, LANES - N_GROUPS - N_EXPERTS), F32)], axis=1)
    wr_hi = wr.astype(BF16)
    wr_lo = (wr - wr_hi.astype(F32)).astype(BF16)
    wr2 = jnp.concatenate([jnp.concatenate([wr_hi, wr_lo], axis=1),
                           jnp.concatenate([wr_hi, jnp.zeros_like(wr_lo)], axis=1)], axis=0)
    br = jnp.concatenate([b_router_group[0], b_router_expert[0],
                          jnp.zeros((LANES - N_GROUPS - N_EXPERTS,), F32)])[None, :].astype(F32)
    w1g = w1[0].astype(BF16)
    w3g = w3[0].astype(BF16)
    w2g = w2[0].astype(BF16).reshape(N_GROUPS, EXPERTS_PER_GROUP * EXPERT_FF, D_MODEL)
    in_wts = (attn_norm[0][None, :], win_ext, q_norm[0][None, :], kv_norm[0][None, :], wuq_ext, w2q)
    out_wts = (wuv_bd, wo, ffn_norm[0][None, :], wr2, br)
    gn = hg_gnorm[0][None, :]
    fin = final_norm[None, :]

    tm_s = 256
    tab_meta = _rope_tables(jnp.arange(N_META))
    tab_p = _rope_tables(N_META + jnp.arange(seq))
    tab_s = tuple(jnp.tile(a, (tm_s // dseq, 1)) for a in _rope_tables(PAST_LEN + jnp.arange(dseq)))
    ckv_m, kr_m, kcat_m, _, hz_m = _inproj(meta_tokens, tab_meta, in_wts, N_META)
    xp = x_prompt.reshape(bp * seq, D_MODEL)
    xs = x_sample.reshape(bs * dseq, D_MODEL)
    ckv_p, kr_p, kcat_p, qcat_p, hz_p = _inproj(xp, tab_p, in_wts, 512)
    ckv_s, kr_s, kcat_s, qcat_s, hz_s = _inproj(xs, tab_s, in_wts, tm_s)

    zero_state = jnp.zeros((1, HG_HEADS, HG_KEY_DIM, HG_VAL_DIM), F32)
    _, st_m = _hgrn(hz_m, lb, gn, zero_state, 1, N_META)
    ohg_p, st_p = _hgrn(hz_p, lb, gn, st_m, bp, 128, 2)
    ohg_s, st_s = _hgrn(hz_s, lb, gn, state_hgrn[0].astype(F32), bs, dseq)

    kmeta = jnp.concatenate([kcat_m, jnp.zeros((LANES - N_META, KCAT), BF16)], axis=0)
    olat_p = _attn_prompt(qcat_p, kmeta, kcat_p, bp, 128, 512)
    n_c = cache_ckv.shape[2]
    kcache = jnp.concatenate([cache_ckv[0], cache_krope[0], jnp.ones((bs, n_c, 1), F32),
                              jnp.zeros((bs, n_c, KCAT - ONE_COL - 1), F32)], axis=-1).astype(BF16)
    olat_s = _attn_sample(qcat_s, kcache, kcat_s, dseq, 2)

    h1_p, hn_p, gate_p = _outproj(xp, olat_p, ohg_p, out_wts, 1024)
    h1_s, hn_s, gate_s = _outproj(xs, olat_s, ohg_s, out_wts, 256)
    y_p = _moe(hn_p, h1_p, gate_p, w1g, w3g, w2g, fin, 512)
    y_s = _moe(hn_s, h1_s, gate_s, w1g, w3g, w2g, fin, 256)

    with_meta = lambda m, a, w: jnp.concatenate(
        [jnp.broadcast_to(m[None], (bp, N_META, w)), a.reshape(bp, seq, w)], axis=1)[None]
    return (y_p.reshape(bp, seq, D_MODEL), y_s.reshape(bs, dseq, D_MODEL),
            with_meta(ckv_m, ckv_p, KV_LORA_RANK), with_meta(kr_m, kr_p, ROPE_DIM), st_p[None],
            ckv_s.reshape(1, bs, dseq, KV_LORA_RANK), kr_s.reshape(1, bs, dseq, ROPE_DIM), st_s[None])
```

```python
import functools

import numpy as np
import jax
import jax.numpy as jnp
from jax import lax
from jax.experimental import pallas as pl
from jax.experimental.pallas import tpu as pltpu

F32 = jnp.float32
BF16 = jnp.bfloat16

D_MODEL = 1024
PAST_LEN = 2048
CHUNK = 64
N_META = 16
EPS = 1e-6
MLA_HEADS = 8
Q_LORA_RANK = 256
KV_LORA_RANK = 128
QK_NOPE_DIM = 64
ROPE_DIM = 32
V_HEAD_DIM = 64
ROPE_THETA = 10000.0
MLA_WIDTH = MLA_HEADS * V_HEAD_DIM
SM_SCALE = (QK_NOPE_DIM + ROPE_DIM) ** -0.5
LOG2E = 1.4426950408889634
Q_SCALE = SM_SCALE * LOG2E
HG_HEADS = 4
HG_KEY_DIM = 128
HG_VAL_DIM = 128
HG_WIDTH = HG_HEADS * HG_VAL_DIM
HG_BLOCK = 16
N_GROUPS = 4
EXPERTS_PER_GROUP = 4
N_EXPERTS = N_GROUPS * EXPERTS_PER_GROUP
EXPERT_FF = 256

LANES = 128
KCAT = 2 * LANES
ONE_COL = LANES + ROPE_DIM
NEG = -1e30
VMEM_LIMIT = 56 * 1024 * 1024
GATE_LANE0 = N_GROUPS
ATTN_UNROLL = 4


def _rms(x, w):
    return x * lax.rsqrt(jnp.mean(x * x, axis=-1, keepdims=True) + EPS) * w


def _dot(a, b):
    return jnp.dot(a, b, preferred_element_type=F32)


def _dot_nt(a, b):
    return lax.dot_general(a, b, (((1,), (1,)), ((), ())), preferred_element_type=F32)


def _dot_tn(a, b):
    return lax.dot_general(a, b, (((0,), (0,)), ((), ())), preferred_element_type=F32)


def _split3(x):
    x1 = x.astype(BF16)
    r = x - x1.astype(F32)
    x2 = r.astype(BF16)
    x3 = (r - x2.astype(F32)).astype(BF16)
    return x1, x2, x3


def _params(n_axes):
    return pltpu.CompilerParams(dimension_semantics=("arbitrary",) * n_axes, vmem_limit_bytes=VMEM_LIMIT)


def _const_spec(shape):
    nd = len(shape)
    return pl.BlockSpec(shape, lambda *_: (0,) * nd)


def _inproj_body(x_ref, cosk_ref, sink_ref, cosq_ref, sinq_ref, an_ref, win_ref, qn_ref, kvn_ref, wuq_ref, w2_ref,
                 ckv_ref, krope_ref, kcat_ref, qcat_ref, hz_ref):
    x = x_ref[...]
    hn = _rms(x, an_ref[...]).astype(BF16)
    z = _dot(hn, win_ref[...])
    hz_ref[...] = z[:, 4 * LANES:]
    ckv = _rms(z[:, 2 * LANES:3 * LANES], kvn_ref[...])
    ckv_ref[...] = ckv
    kr = z[:, 3 * LANES:4 * LANES]
    rot = kr * cosk_ref[...] + pltpu.roll(kr, LANES - ROPE_DIM, 1) * sink_ref[...]
    krope_ref[...] = rot[:, :ROPE_DIM]
    lane = lax.broadcasted_iota(jnp.int32, rot.shape, 1)
    kcat_ref[...] = jnp.concatenate([ckv, jnp.where(lane == ROPE_DIM, 1.0, rot)], axis=-1).astype(BF16)
    cqn = _rms(z[:, :Q_LORA_RANK], qn_ref[...]).astype(BF16)
    qz = _dot(cqn, wuq_ref[...])
    cosq = cosq_ref[...]
    sinq = sinq_ref[...]
    for h in range(MLA_HEADS):
        blk = qz[:, h * LANES:(h + 1) * LANES]
        r = blk * cosq + pltpu.roll(blk, LANES - ROPE_DIM, 1) * sinq
        qcat_ref[h] = (_dot(r.astype(BF16), w2_ref[h]) * Q_SCALE).astype(BF16)


def _inproj(x2d, tabs, wts, tm):
    t = x2d.shape[0]
    period = tabs[0].shape[0] // tm
    grid = (t // tm,)
    row = lambda w: pl.BlockSpec((tm, w), lambda i: (i, 0))
    tab = pl.BlockSpec((tm, LANES), lambda i: (i % period, 0))
    an, win, qn, kvn, wuq, w2 = wts
    return pl.pallas_call(
        _inproj_body,
        grid=grid,
        in_specs=[row(D_MODEL), tab, tab, tab, tab, _const_spec(an.shape), _const_spec(win.shape),
                  _const_spec(qn.shape), _const_spec(kvn.shape), _const_spec(wuq.shape), _const_spec(w2.shape)],
        out_specs=[row(KV_LORA_RANK), row(ROPE_DIM), row(KCAT),
                   pl.BlockSpec((MLA_HEADS, tm, KCAT), lambda i: (0, i, 0)), row(4 * HG_WIDTH)],
        out_shape=[jax.ShapeDtypeStruct((t, KV_LORA_RANK), F32), jax.ShapeDtypeStruct((t, ROPE_DIM), F32),
                   jax.ShapeDtypeStruct((t, KCAT), BF16), jax.ShapeDtypeStruct((MLA_HEADS, t, KCAT), BF16),
                   jax.ShapeDtypeStruct((t, 4 * HG_WIDTH), F32)],
        compiler_params=_params(1),
        name="inproj",
    )(x2d, *tabs, an, win, qn, kvn, wuq, w2)


def _hgrn_body(hq_ref, hf_ref, hi_ref, hg_ref, lb_ref, gn_ref, s0_ref, tri_ref, lv_ref, cst_ref,
               o_ref, sout_ref, st_ref, *, tt, n_sub):
    step = pl.program_id(1)
    n_levels = tt.bit_length() - 1

    @pl.when(step == 0)
    def _():
        for h in range(HG_HEADS):
            st_ref[h] = s0_ref[0, h].T

    lb = lb_ref[...]
    gn = gn_ref[...]
    tri = tri_ref[...]
    lv = lv_ref[...]
    for sub in range(n_sub):
        rows = slice(sub * tt, (sub + 1) * tt)
        hq = hq_ref[rows, :]
        q = hq * jax.nn.sigmoid(hq) * (HG_KEY_DIM ** -0.5)
        f = lb + (1.0 - lb) * jax.nn.sigmoid(hf_ref[rows, :])
        g = jnp.log(f) * LOG2E
        kk = 1.0 - f
        v = hi_ref[rows, :]
        g1, g2, g3 = _split3(g)
        bc = _dot(tri, g1) + _dot(tri, g2) + _dot(tri, g3)

        bl = bc[tt - 1:tt, :]
        qs = (q * jnp.exp2(bc)).astype(BF16)
        kf = (kk * jnp.exp2(bl - bc)).astype(BF16)
        vb = v.astype(BF16)
        dl = jnp.exp2(bl)
        hg = hg_ref[rows, :]
        for h in range(HG_HEADS):
            cols = slice(h * LANES, (h + 1) * LANES)
            qh, kh, gh, bch = q[:, cols], kk[:, cols], g[:, cols], bc[:, cols]
            pairs = [(qh, kh)]
            pairs.append((qh * jnp.exp2(gh + cst_ref[0]), kh * cst_ref[1]))
            if n_levels > 1:
                g_prev = pltpu.roll(gh, 1, 0)
                g_next = pltpu.roll(gh, tt - 1, 0)
                pairs.append((qh * jnp.exp2(gh + g_prev * cst_ref[2] + cst_ref[3]),
                              kh * jnp.exp2(g_next * cst_ref[4] + cst_ref[5])))
            for lvl in range(2, n_levels):
                grp = 2 << lvl
                mid = bch.reshape(tt // grp, grp, LANES)[:, grp // 2 - 1:grp // 2, :]
                d = bch - jnp.broadcast_to(mid, (tt // grp, grp, LANES)).reshape(tt, LANES)
                pairs.append((qh * jnp.exp2(d + cst_ref[2 + 2 * lvl]), kh * jnp.exp2(cst_ref[3 + 2 * lvl] - d)))
            att = jnp.zeros((tt, tt), F32)
            for idx, (qa, ka) in enumerate(pairs):
                att = jnp.where(lv == idx - 1, _dot_nt(qa.astype(BF16), ka.astype(BF16)), att)
            st = st_ref[h]
            o = _dot(att.astype(BF16), vb[:, cols]) + _dot_nt(qs[:, cols], st.astype(BF16))
            st_ref[h] = dl[:, cols] * st + _dot_tn(vb[:, cols], kf[:, cols])
            gate = hg[:, cols]
            o_ref[rows, cols] = (_rms(o, gn) * (gate * jax.nn.sigmoid(gate))).astype(BF16)

    @pl.when(step == pl.num_programs(1) - 1)
    def _():
        for h in range(HG_HEADS):
            sout_ref[0, h] = st_ref[h].T


def _hgrn(hz, lb, gn, s0, batch, tt, n_sub=1):
    t_all = hz.shape[0]
    blk = tt * n_sub
    steps = t_all // batch // blk
    idx = np.arange(tt)
    tri = jnp.asarray(idx[None, :] <= idx[:, None], BF16)
    diff = idx[:, None] ^ idx[None, :]
    level = np.where(diff > 0, np.floor(np.log2(np.maximum(diff, 1))), -1).astype(np.int32)
    lv = jnp.asarray(np.where(idx[None, :] <= idx[:, None], level, tt), jnp.int32)
    rows = np.broadcast_to(idx[:, None], (tt, LANES))
    bias = lambda keep: np.where(keep, 0.0, NEG)
    one = lambda keep: np.where(keep, 1.0, 0.0)
    cst = [bias(rows % 2 == 1), one(rows % 2 == 0),
           one(rows % 4 == 3), bias(rows % 4 >= 2), one(rows % 4 == 0), bias(rows % 4 < 2)]
    for lvl in range(2, tt.bit_length() - 1):
        upper = (rows >> lvl) % 2 == 1
        cst += [bias(upper), bias(~upper)]
    cst = jnp.asarray(np.stack(cst), F32)
    col = lambda c: pl.BlockSpec((blk, HG_WIDTH), lambda b, s: (b * steps + s, c))
    shared_s0 = s0.shape[0] == 1
    state = lambda shared: pl.BlockSpec((1, HG_HEADS, HG_KEY_DIM, HG_VAL_DIM),
                                        (lambda b, s: (0, 0, 0, 0)) if shared else (lambda b, s: (b, 0, 0, 0)))
    return pl.pallas_call(
        functools.partial(_hgrn_body, tt=tt, n_sub=n_sub),
        grid=(batch, steps),
        in_specs=[col(0), col(1), col(2), col(3), _const_spec(lb.shape), _const_spec(gn.shape), state(shared_s0),
                  _const_spec(tri.shape), _const_spec(lv.shape), _const_spec(cst.shape)],
        out_specs=[pl.BlockSpec((blk, HG_WIDTH), lambda b, s: (b * steps + s, 0)), state(False)],
        out_shape=[jax.ShapeDtypeStruct((t_all, HG_WIDTH), BF16),
                   jax.ShapeDtypeStruct((batch, HG_HEADS, HG_KEY_DIM, HG_VAL_DIM), F32)],
        scratch_shapes=[pltpu.VMEM((HG_HEADS, HG_VAL_DIM, HG_KEY_DIM), F32)],
        compiler_params=_params(2),
        name="hgrn",
    )(hz, hz, hz, hz, lb, gn, s0, tri, lv, cst)


def _lanes(x, n):
    return jnp.concatenate([x] * n, axis=1)


def _softmax_update(s, kt, m_scr, acc_scr):
    m_prev = m_scr[...]
    m_new = jnp.maximum(m_prev, jnp.max(s, axis=1, keepdims=True))
    alpha = jnp.exp2(m_prev - m_new)
    p = jnp.exp2(s - _lanes(m_new, s.shape[1] // LANES))
    acc_scr[...] = _lanes(alpha, KCAT // LANES) * acc_scr[...] + _dot(p.astype(BF16), kt)
    m_scr[...] = m_new


def _attn_finish(acc, o_ref, tq, row0=0):
    o = acc[:, :KV_LORA_RANK] / acc[:, ONE_COL:ONE_COL + 1]
    for h in range(MLA_HEADS):
        o_ref[row0:row0 + tq, h * LANES:(h + 1) * LANES] = o[h * tq:(h + 1) * tq].astype(BF16)


def _attn_prompt_body(q_ref, qn_ref, kmeta_ref, k_ref, o_ref, m_scr, acc_scr, sl_scr, sm_scr, sa_scr, sb_scr,
                      *, tq, tk):
    i = pl.program_id(1)
    rows = MLA_HEADS * tq
    n_tiles = k_ref.shape[0] // tk
    q = q_ref[...].reshape(rows, KCAT)
    kmeta = kmeta_ref[...]
    n = (i * tq) // tk

    def keys(t):
        return k_ref[pl.ds(pl.multiple_of(t * tk, tk), tk), :]

    def scores(qv, t, dst):
        dst[...] = _dot_nt(qv, keys(t))

    def consume(src, t):
        _softmax_update(src[...], keys(t), m_scr, acc_scr)

    def first_stage(qv, t):
        scores(qv, t, sl_scr)
        sm_scr[...] = _dot_nt(qv, kmeta)

    def prefetch():
        first_stage(qn_ref[...].reshape(rows, KCAT), jnp.minimum(((i + 1) * tq) // tk, n_tiles - 1))

    @pl.when(i == 0)
    def _():
        first_stage(q, 0)

    scores(q, 0, sa_scr)
    limit = ((i * tq + lax.broadcasted_iota(jnp.int32, (tq, tk), 0)) // CHUNK + 1) * CHUNK - n * tk
    bias = jnp.where(lax.broadcasted_iota(jnp.int32, (tq, tk), 1) < limit, 0.0, NEG)
    s_last = sl_scr[...] + jnp.concatenate([bias] * MLA_HEADS, axis=0)
    bias_meta = jnp.where(lax.broadcasted_iota(jnp.int32, (tq, LANES), 1) < N_META, 0.0, NEG)
    s_meta = sm_scr[...] + jnp.concatenate([bias_meta] * MLA_HEADS, axis=0)
    m0 = jnp.maximum(jnp.max(s_last, axis=1, keepdims=True), jnp.max(s_meta, axis=1, keepdims=True))
    m0 = jnp.broadcast_to(m0, (rows, LANES))
    p_last = jnp.exp2(s_last - _lanes(m0, tk // LANES)).astype(BF16)
    p_meta = jnp.exp2(s_meta - m0).astype(BF16)
    acc_scr[...] = _dot(p_last, keys(n)) + _dot(p_meta, kmeta)
    m_scr[...] = m0

    bufs = (sa_scr, sb_scr)

    def run(first, count, last):
        for j in range(count):
            if j + 1 < count or not last:
                scores(q, first + j + 1, bufs[(j + 1) % 2])
            else:
                prefetch()
            consume(bufs[j % 2], first + j)

    def body(p, carry):
        run(ATTN_UNROLL * p, ATTN_UNROLL, False)
        return carry

    trips = jnp.maximum(n - 1, 0) // ATTN_UNROLL
    lax.fori_loop(0, trips, body, 0)
    base = ATTN_UNROLL * trips

    @pl.when(n == 0)
    def _():
        prefetch()

    for count in range(1, ATTN_UNROLL + 1):
        @pl.when(n - base == count)
        def _(count=count):
            run(base, count, True)

    _attn_finish(acc_scr[...], o_ref, tq)


def _attn_prompt(qcat, kmeta, kcat, batch, tq, tk):
    t_all = kcat.shape[0]
    seq = t_all // batch
    nq = seq // tq
    rows = MLA_HEADS * tq
    last_block = t_all // tq - 1
    return pl.pallas_call(
        functools.partial(_attn_prompt_body, tq=tq, tk=tk),
        grid=(batch, nq),
        in_specs=[pl.BlockSpec((MLA_HEADS, tq, KCAT), lambda b, i: (0, b * nq + i, 0)),
                  pl.BlockSpec((MLA_HEADS, tq, KCAT), lambda b, i: (0, jnp.minimum(b * nq + i + 1, last_block), 0)),
                  _const_spec(kmeta.shape),
                  pl.BlockSpec((seq, KCAT), lambda b, i: (b, 0))],
        out_specs=pl.BlockSpec((tq, MLA_HEADS * LANES), lambda b, i: (b * nq + i, 0)),
        out_shape=jax.ShapeDtypeStruct((t_all, MLA_HEADS * LANES), BF16),
        scratch_shapes=[pltpu.VMEM((rows, LANES), F32), pltpu.VMEM((rows, KCAT), F32),
                        pltpu.VMEM((rows, tk), F32), pltpu.VMEM((rows, LANES), F32),
                        pltpu.VMEM((rows, tk), F32), pltpu.VMEM((rows, tk), F32)],
        compiler_params=_params(2),
        name="attn_prompt",
    )(qcat, qcat, kmeta, kcat)


def _attn_sample_body(q_ref, kc_ref, kn_ref, o_ref, *, tq, ns):
    rows = MLA_HEADS * tq
    for s in range(ns):
        q = q_ref[:, s * tq:(s + 1) * tq, :].reshape(rows, KCAT)
        kc = kc_ref[s]
        kn = kn_ref[s * tq:(s + 1) * tq, :]
        s_c = _dot_nt(q, kc)
        s_n = _dot_nt(q, kn)
        m = jnp.maximum(jnp.max(s_c, axis=1, keepdims=True), jnp.max(s_n, axis=1, keepdims=True))
        acc = _dot(jnp.exp2(s_c - m).astype(BF16), kc) + _dot(jnp.exp2(s_n - m).astype(BF16), kn)
        _attn_finish(acc, o_ref, tq, s * tq)


def _attn_sample(qcat, kcache, kcat, tq, ns):
    batch = kcache.shape[0]
    t_all = kcat.shape[0]
    return pl.pallas_call(
        functools.partial(_attn_sample_body, tq=tq, ns=ns),
        grid=(batch // ns,),
        in_specs=[pl.BlockSpec((MLA_HEADS, ns * tq, KCAT), lambda b: (0, b, 0)),
                  pl.BlockSpec((ns,) + kcache.shape[1:], lambda b: (b, 0, 0)),
                  pl.BlockSpec((ns * tq, KCAT), lambda b: (b, 0))],
        out_specs=pl.BlockSpec((ns * tq, MLA_HEADS * LANES), lambda b: (b, 0)),
        out_shape=jax.ShapeDtypeStruct((t_all, MLA_HEADS * LANES), BF16),
        compiler_params=_params(1),
        name="attn_sample",
    )(qcat, kcache, kcat)


def _outproj_body(x_ref, olat_ref, ohg_ref, wuv_ref, wo_ref, fn_ref, wr_ref, br_ref,
                  h1_ref, hn_ref, gate_ref, *, n_split):
    tm = x_ref.shape[0] // n_split
    for r in range(n_split):
        rows = slice(r * tm, (r + 1) * tm)
        _outproj_rows(x_ref.at[rows], olat_ref.at[rows], ohg_ref.at[rows], wuv_ref, wo_ref, fn_ref, wr_ref,
                      br_ref, h1_ref.at[rows], hn_ref.at[rows], gate_ref.at[rows])


def _outproj_rows(x_ref, olat_ref, ohg_ref, wuv_ref, wo_ref, fn_ref, wr_ref, br_ref, h1_ref, hn_ref, gate_ref):
    o_mla = _dot(olat_ref[...], wuv_ref[...]).astype(BF16)
    mix = _dot(o_mla, wo_ref[:MLA_WIDTH, :]) + _dot(ohg_ref[...], wo_ref[MLA_WIDTH:, :])
    h1 = x_ref[...] + mix
    h1_ref[...] = h1
    hn = _rms(h1, fn_ref[...])
    hn_hi = hn.astype(BF16)
    hn_ref[...] = hn_hi
    hn_lo = (hn - hn_hi.astype(F32)).astype(BF16)
    r2 = _dot(jnp.concatenate([hn_hi, hn_lo], axis=1), wr_ref[...])
    logits = r2[:, :LANES] + r2[:, LANES:] + br_ref[...]
    lane = lax.broadcasted_iota(jnp.int32, logits.shape, 1)
    gl = jnp.where(lane < N_GROUPS, logits, NEG)
    gmax = jnp.max(gl, axis=1, keepdims=True)
    g_top = 1.0 / jnp.sum(jnp.exp(gl - gmax), axis=1, keepdims=True)
    g_idx = jnp.min(jnp.where(gl == gmax, lane, LANES), axis=1, keepdims=True)
    lo = GATE_LANE0 + EXPERTS_PER_GROUP * g_idx
    el = jnp.where((lane >= lo) & (lane < lo + EXPERTS_PER_GROUP), logits, NEG)
    e1 = jnp.max(el, axis=1, keepdims=True)
    i1 = jnp.min(jnp.where(el == e1, lane, LANES), axis=1, keepdims=True)
    el2 = jnp.where(lane == i1, NEG, el)
    e2 = jnp.max(el2, axis=1, keepdims=True)
    i2 = jnp.min(jnp.where(el2 == e2, lane, LANES), axis=1, keepdims=True)
    esum = jnp.sum(jnp.exp(el - e1), axis=1, keepdims=True)
    p1 = 1.0 / esum
    p2 = jnp.exp(e2 - e1) / esum
    w1 = p1 / (p1 + p2) * g_top
    w2 = p2 / (p1 + p2) * g_top
    gate_ref[...] = jnp.where(lane == i1, w1, jnp.where(lane == i2, w2, 0.0))


def _outproj(x2d, olat, ohg, wts, tm):
    t = x2d.shape[0]
    row = lambda w: pl.BlockSpec((tm, w), lambda i: (i, 0))
    return pl.pallas_call(
        functools.partial(_outproj_body, n_split=max(1, tm // 256)),
        grid=(t // tm,),
        in_specs=[row(D_MODEL), row(MLA_HEADS * LANES), row(HG_WIDTH)] + [_const_spec(w.shape) for w in wts],
        out_specs=[row(D_MODEL), row(D_MODEL), row(LANES)],
        out_shape=[jax.ShapeDtypeStruct((t, D_MODEL), F32), jax.ShapeDtypeStruct((t, D_MODEL), BF16),
                   jax.ShapeDtypeStruct((t, LANES), F32)],
        compiler_params=_params(1),
        name="outproj",
    )(x2d, olat, ohg, *wts)


def _moe_body(hn_ref, h1_ref, gate_ref, w1_ref, w3_ref, w2_ref, fin_ref, y_ref):
    t = hn_ref[...]
    gates = gate_ref[...]
    acc = jnp.zeros(h1_ref.shape, F32)
    for grp in range(N_GROUPS):
        parts = []
        for j in range(EXPERTS_PER_GROUP):
            e = grp * EXPERTS_PER_GROUP + j
            a = _dot(t, w1_ref[e])
            c = _dot(t, w3_ref[e])
            lane = GATE_LANE0 + e
            parts.append((a * jax.nn.sigmoid(a) * c * gates[:, lane:lane + 1]).astype(BF16))
        acc = acc + _dot(jnp.concatenate(parts, axis=-1), w2_ref[grp])
    y_ref[...] = _rms(h1_ref[...] + acc, fin_ref[...])


def _moe(hn, h1, gates, w1g, w3g, w2g, fin, tm):
    t = hn.shape[0]
    row = lambda w: pl.BlockSpec((tm, w), lambda i: (i, 0))
    resident = lambda a: pl.BlockSpec(a.shape, lambda i: (0,) * a.ndim, pipeline_mode=pl.Buffered(1))
    return pl.pallas_call(
        _moe_body,
        grid=(t // tm,),
        in_specs=[row(D_MODEL), row(D_MODEL), row(LANES), resident(w1g), resident(w3g), resident(w2g),
                  _const_spec(fin.shape)],
        out_specs=row(D_MODEL),
        out_shape=jax.ShapeDtypeStruct((t, D_MODEL), F32),
        compiler_params=_params(1),
        name="moe",
    )(hn, h1, gates, w1g, w3g, w2g, fin)


def _rope_tables(pos):
    half = ROPE_DIM // 2
    inv = ROPE_THETA ** (-jnp.arange(half, dtype=F32) / half)
    ang = pos.astype(F32)[:, None] * inv[None, :]
    cos, sin = jnp.cos(ang), jnp.sin(ang)
    n = pos.shape[0]
    z = lambda w: jnp.zeros((n, w), F32)
    cosk = jnp.concatenate([cos, cos, z(LANES - ROPE_DIM)], axis=1)
    sink = jnp.concatenate([-sin, sin, z(LANES - ROPE_DIM)], axis=1)
    cosq = jnp.concatenate([jnp.ones((n, QK_NOPE_DIM), F32), cos, cos, z(ROPE_DIM)], axis=1)
    sinq = jnp.concatenate([z(QK_NOPE_DIM), -sin, sin, z(ROPE_DIM)], axis=1)
    return cosk, sink, cosq, sinq


def kernel(x_prompt, x_sample, cache_ckv, cache_krope, state_hgrn, meta_tokens, attn_norm, w_in, q_norm, kv_norm, w_uq, w_uk, w_uv, hg_lb_logits, hg_gnorm, w_o, ffn_norm, w_router_group, b_router_group, w_router_expert, b_router_expert, w1, w3, w2, final_norm):
    bp, seq, _ = x_prompt.shape
    bs, dseq, _ = x_sample.shape
    half = ROPE_DIM // 2
    swap = np.concatenate([np.arange(half, ROPE_DIM), np.arange(half)])

    lb = jax.nn.softmax(hg_lb_logits.astype(F32), axis=0)[0][None, :]
    wi = w_in[0]
    o_kv, o_kr, o_hg = Q_LORA_RANK, Q_LORA_RANK + KV_LORA_RANK, Q_LORA_RANK + KV_LORA_RANK + ROPE_DIM
    w_kr = wi[:, o_kr:o_hg]
    win_ext = jnp.concatenate([wi[:, :o_kr], w_kr, w_kr[:, swap], jnp.zeros((D_MODEL, LANES - 2 * ROPE_DIM), F32),
                               wi[:, o_hg:]], axis=1).astype(BF16)
    wq = w_uq[0]
    wq_rope = wq[:, :, QK_NOPE_DIM:]
    wuq_ext = jnp.concatenate([wq, wq_rope[:, :, swap]], axis=2).reshape(Q_LORA_RANK, MLA_HEADS * LANES).astype(BF16)
    eye = jnp.eye(ROPE_DIM, dtype=F32)
    w2q = jnp.zeros((MLA_HEADS, LANES, KCAT), F32)
    w2q = w2q.at[:, :QK_NOPE_DIM, :KV_LORA_RANK].set(jnp.transpose(w_uk[0], (1, 2, 0)))
    w2q = w2q.at[:, QK_NOPE_DIM:QK_NOPE_DIM + ROPE_DIM, KV_LORA_RANK:KV_LORA_RANK + ROPE_DIM].set(eye[None])
    w2q = w2q.astype(BF16)
    wuv_bd = jnp.zeros((MLA_HEADS, KV_LORA_RANK, MLA_HEADS, V_HEAD_DIM), F32)
    for h in range(MLA_HEADS):
        wuv_bd = wuv_bd.at[h, :, h, :].set(w_uv[0][:, h, :])
    wuv_bd = wuv_bd.reshape(MLA_HEADS * KV_LORA_RANK, MLA_WIDTH).astype(BF16)
    wo = w_o[0].astype(BF16)
    wr = jnp.concatenate([w_router_group[0], w_router_expert[0],
                          jnp.zeros((D_MODEL, LANES - N_GROUPS - N_EXPERTS), F32)], axis=1)
    wr_hi = wr.astype(BF16)
    wr_lo = (wr - wr_hi.astype(F32)).astype(BF16)
    wr2 = jnp.concatenate([jnp.concatenate([wr_hi, wr_lo], axis=1),
                           jnp.concatenate([wr_hi, jnp.zeros_like(wr_lo)], axis=1)], axis=0)
    br = jnp.concatenate([b_router_group[0], b_router_expert[0],
                          jnp.zeros((LANES - N_GROUPS - N_EXPERTS,), F32)])[None, :].astype(F32)
    w1g = w1[0].astype(BF16)
    w3g = w3[0].astype(BF16)
    w2g = w2[0].astype(BF16).reshape(N_GROUPS, EXPERTS_PER_GROUP * EXPERT_FF, D_MODEL)
    in_wts = (attn_norm[0][None, :], win_ext, q_norm[0][None, :], kv_norm[0][None, :], wuq_ext, w2q)
    out_wts = (wuv_bd, wo, ffn_norm[0][None, :], wr2, br)
    gn = hg_gnorm[0][None, :]
    fin = final_norm[None, :]

    tm_s = 256
    tab_meta = _rope_tables(jnp.arange(N_META))
    tab_p = _rope_tables(N_META + jnp.arange(seq))
    tab_s = tuple(jnp.tile(a, (tm_s // dseq, 1)) for a in _rope_tables(PAST_LEN + jnp.arange(dseq)))
    ckv_m, kr_m, kcat_m, _, hz_m = _inproj(meta_tokens, tab_meta, in_wts, N_META)
    xp = x_prompt.reshape(bp * seq, D_MODEL)
    xs = x_sample.reshape(bs * dseq, D_MODEL)
    ckv_p, kr_p, kcat_p, qcat_p, hz_p = _inproj(xp, tab_p, in_wts, 512)
    ckv_s, kr_s, kcat_s, qcat_s, hz_s = _inproj(xs, tab_s, in_wts, tm_s)

    zero_state = jnp.zeros((1, HG_HEADS, HG_KEY_DIM, HG_VAL_DIM), F32)
    _, st_m = _hgrn(hz_m, lb, gn, zero_state, 1, N_META)
    ohg_p, st_p = _hgrn(hz_p, lb, gn, st_m, bp, 128, 4)
    ohg_s, st_s = _hgrn(hz_s, lb, gn, state_hgrn[0].astype(F32), bs, dseq)

    kmeta = jnp.concatenate([kcat_m, jnp.zeros((LANES - N_META, KCAT), BF16)], axis=0)
    olat_p = _attn_prompt(qcat_p, kmeta, kcat_p, bp, 128, 512)
    n_c = cache_ckv.shape[2]
    kcache = jnp.concatenate([cache_ckv[0], cache_krope[0], jnp.ones((bs, n_c, 1), F32),
                              jnp.zeros((bs, n_c, KCAT - ONE_COL - 1), F32)], axis=-1).astype(BF16)
    olat_s = _attn_sample(qcat_s, kcache, kcat_s, dseq, 4)

    h1_p, hn_p, gate_p = _outproj(xp, olat_p, ohg_p, out_wts, 1024)
    h1_s, hn_s, gate_s = _outproj(xs, olat_s, ohg_s, out_wts, 256)
    y_p = _moe(hn_p, h1_p, gate_p, w1g, w3g, w2g, fin, 512)
    y_s = _moe(hn_s, h1_s, gate_s, w1g, w3g, w2g, fin, 256)

    with_meta = lambda m, a, w: jnp.concatenate(
        [jnp.broadcast_to(m[None], (bp, N_META, w)), a.reshape(bp, seq, w)], axis=1)[None]
    return (y_p.reshape(bp, seq, D_MODEL), y_s.reshape(bs, dseq, D_MODEL),
            with_meta(ckv_m, ckv_p, KV_LORA_RANK), with_meta(kr_m, kr_p, ROPE_DIM), st_p[None],
            ckv_s.reshape(1, bs, dseq, KV_LORA_RANK), kr_s.reshape(1, bs, dseq, ROPE_DIM), st_s[None])
```

```python
import functools

import numpy as np
import jax
import jax.numpy as jnp
from jax import lax
from jax.experimental import pallas as pl
from jax.experimental.pallas import tpu as pltpu

F32 = jnp.float32
BF16 = jnp.bfloat16

D_MODEL = 1024
PAST_LEN = 2048
CHUNK = 64
N_META = 16
EPS = 1e-6
MLA_HEADS = 8
Q_LORA_RANK = 256
KV_LORA_RANK = 128
QK_NOPE_DIM = 64
ROPE_DIM = 32
V_HEAD_DIM = 64
ROPE_THETA = 10000.0
MLA_WIDTH = MLA_HEADS * V_HEAD_DIM
SM_SCALE = (QK_NOPE_DIM + ROPE_DIM) ** -0.5
LOG2E = 1.4426950408889634
Q_SCALE = SM_SCALE * LOG2E
HG_HEADS = 4
HG_KEY_DIM = 128
HG_VAL_DIM = 128
HG_WIDTH = HG_HEADS * HG_VAL_DIM
HG_BLOCK = 16
N_GROUPS = 4
EXPERTS_PER_GROUP = 4
N_EXPERTS = N_GROUPS * EXPERTS_PER_GROUP
EXPERT_FF = 256

LANES = 128
KCAT = 2 * LANES
ONE_COL = LANES + ROPE_DIM
NEG = -1e30
VMEM_LIMIT = 56 * 1024 * 1024
GATE_LANE0 = N_GROUPS
ATTN_UNROLL = 4


def _rms(x, w):
    return x * lax.rsqrt(jnp.mean(x * x, axis=-1, keepdims=True) + EPS) * w


def _dot(a, b):
    return jnp.dot(a, b, preferred_element_type=F32)


def _dot_nt(a, b):
    return lax.dot_general(a, b, (((1,), (1,)), ((), ())), preferred_element_type=F32)


def _dot_tn(a, b):
    return lax.dot_general(a, b, (((0,), (0,)), ((), ())), preferred_element_type=F32)


def _split3(x):
    x1 = x.astype(BF16)
    r = x - x1.astype(F32)
    x2 = r.astype(BF16)
    x3 = (r - x2.astype(F32)).astype(BF16)
    return x1, x2, x3


def _params(n_axes):
    return pltpu.CompilerParams(dimension_semantics=("arbitrary",) * n_axes, vmem_limit_bytes=VMEM_LIMIT)


def _const_spec(shape):
    nd = len(shape)
    return pl.BlockSpec(shape, lambda *_: (0,) * nd)


def _inproj_stages(x_ref, cosk_ref, sink_ref, cosq_ref, sinq_ref, an_ref, win_ref, qn_ref, kvn_ref, wuq_ref, w2_ref,
                   ckv_ref, krope_ref, kcat_ref, qcat_ref, hz_ref):
    env = {}
    lat = 4 * LANES

    def latent():
        hn = _rms(x_ref[...], an_ref[...]).astype(BF16)
        env["hn"] = hn
        z = _dot(hn, win_ref[:, :lat])
        ckv = _rms(z[:, 2 * LANES:3 * LANES], kvn_ref[...])
        ckv_ref[...] = ckv
        kr = z[:, 3 * LANES:]
        rot = kr * cosk_ref[...] + pltpu.roll(kr, LANES - ROPE_DIM, 1) * sink_ref[...]
        krope_ref[...] = rot[:, :ROPE_DIM]
        lane = lax.broadcasted_iota(jnp.int32, rot.shape, 1)
        kcat_ref[...] = jnp.concatenate([ckv, jnp.where(lane == ROPE_DIM, 1.0, rot)], axis=-1).astype(BF16)
        env["cqn"] = _rms(z[:, :Q_LORA_RANK], qn_ref[...]).astype(BF16)

    def gates(c):
        def stage():
            cols = slice(c * KCAT, (c + 1) * KCAT)
            hz_ref[:, cols] = _dot(env["hn"], win_ref[:, lat + c * KCAT:lat + (c + 1) * KCAT])
        return stage

    def queries_up():
        env["qz"] = _dot(env["cqn"], wuq_ref[...])

    def query_head(h):
        def stage():
            blk = env["qz"][:, h * LANES:(h + 1) * LANES]
            r = blk * cosq_ref[...] + pltpu.roll(blk, LANES - ROPE_DIM, 1) * sinq_ref[...]
            qcat_ref[h] = (_dot(r.astype(BF16), w2_ref[h]) * Q_SCALE).astype(BF16)
        return stage

    return ([latent] + [gates(c) for c in range(4 * HG_WIDTH // KCAT)] + [queries_up]
            + [query_head(h) for h in range(MLA_HEADS)])


def _inproj_body(*refs):
    for stage in _inproj_stages(*refs):
        stage()


def _inproj(x2d, tabs, wts, tm):
    t = x2d.shape[0]
    period = tabs[0].shape[0] // tm
    grid = (t // tm,)
    row = lambda w: pl.BlockSpec((tm, w), lambda i: (i, 0))
    tab = pl.BlockSpec((tm, LANES), lambda i: (i % period, 0))
    an, win, qn, kvn, wuq, w2 = wts
    return pl.pallas_call(
        _inproj_body,
        grid=grid,
        in_specs=[row(D_MODEL), tab, tab, tab, tab, _const_spec(an.shape), _const_spec(win.shape),
                  _const_spec(qn.shape), _const_spec(kvn.shape), _const_spec(wuq.shape), _const_spec(w2.shape)],
        out_specs=[row(KV_LORA_RANK), row(ROPE_DIM), row(KCAT),
                   pl.BlockSpec((MLA_HEADS, tm, KCAT), lambda i: (0, i, 0)), row(4 * HG_WIDTH)],
        out_shape=[jax.ShapeDtypeStruct((t, KV_LORA_RANK), F32), jax.ShapeDtypeStruct((t, ROPE_DIM), F32),
                   jax.ShapeDtypeStruct((t, KCAT), BF16), jax.ShapeDtypeStruct((MLA_HEADS, t, KCAT), BF16),
                   jax.ShapeDtypeStruct((t, 4 * HG_WIDTH), F32)],
        compiler_params=_params(1),
        name="inproj",
    )(x2d, *tabs, an, win, qn, kvn, wuq, w2)


def _hgrn_body(hq_ref, hf_ref, hi_ref, hg_ref, lb_ref, gn_ref, s0_ref, tri_ref, lv_ref, cst_ref,
               o_ref, sout_ref, st_ref, *, tt, n_sub):
    step = pl.program_id(1)

    @pl.when(step == 0)
    def _():
        for h in range(HG_HEADS):
            st_ref[h] = s0_ref[0, h].T

    _hgrn_tiles(hq_ref, hf_ref, hi_ref, hg_ref, lb_ref, gn_ref, tri_ref, lv_ref, cst_ref, o_ref, st_ref,
                tt=tt, n_sub=n_sub)

    @pl.when(step == pl.num_programs(1) - 1)
    def _():
        for h in range(HG_HEADS):
            sout_ref[0, h] = st_ref[h].T


def _hgrn_tiles(*refs, tt, n_sub):
    for stage in _hgrn_stages(*refs, tt=tt, n_sub=n_sub):
        stage()


def _hgrn_stages(hq_ref, hf_ref, hi_ref, hg_ref, lb_ref, gn_ref, tri_ref, lv_ref, cst_ref, o_ref, st_ref, *, tt, n_sub):
    n_levels = tt.bit_length() - 1
    lb = lb_ref[...]
    gn = gn_ref[...]
    tri = tri_ref[...]
    lv = lv_ref[...]
    def tile(sub):
        rows = slice(sub * tt, (sub + 1) * tt)
        env = {}

        def prologue():
            hq = hq_ref[rows, :]
            q = hq * jax.nn.sigmoid(hq) * (HG_KEY_DIM ** -0.5)
            f = lb + (1.0 - lb) * jax.nn.sigmoid(hf_ref[rows, :])
            g = jnp.log(f) * LOG2E
            kk = 1.0 - f
            g1, g2, g3 = _split3(g)
            bc = _dot(tri, g1) + _dot(tri, g2) + _dot(tri, g3)
            bl = bc[tt - 1:tt, :]
            env.update(q=q, kk=kk, g=g, bc=bc, qs=(q * jnp.exp2(bc)).astype(BF16),
                       kf=(kk * jnp.exp2(bl - bc)).astype(BF16), vb=hi_ref[rows, :].astype(BF16), dl=jnp.exp2(bl))

        def head(h):
            def stage():
                cols = slice(h * LANES, (h + 1) * LANES)
                qh, kh, gh, bch = env["q"][:, cols], env["kk"][:, cols], env["g"][:, cols], env["bc"][:, cols]
                pairs = [(qh, kh)]
                pairs.append((qh * jnp.exp2(gh + cst_ref[0]), kh * cst_ref[1]))
                if n_levels > 1:
                    g_prev = pltpu.roll(gh, 1, 0)
                    g_next = pltpu.roll(gh, tt - 1, 0)
                    pairs.append((qh * jnp.exp2(gh + g_prev * cst_ref[2] + cst_ref[3]),
                                  kh * jnp.exp2(g_next * cst_ref[4] + cst_ref[5])))
                for lvl in range(2, n_levels):
                    grp = 2 << lvl
                    mid = bch.reshape(tt // grp, grp, LANES)[:, grp // 2 - 1:grp // 2, :]
                    d = bch - jnp.broadcast_to(mid, (tt // grp, grp, LANES)).reshape(tt, LANES)
                    pairs.append((qh * jnp.exp2(d + cst_ref[2 + 2 * lvl]), kh * jnp.exp2(cst_ref[3 + 2 * lvl] - d)))
                att = jnp.zeros((tt, tt), F32)
                for idx, (qa, ka) in enumerate(pairs):
                    att = jnp.where(lv == idx - 1, _dot_nt(qa.astype(BF16), ka.astype(BF16)), att)
                st = st_ref[h]
                vh = env["vb"][:, cols]
                o = _dot(att.astype(BF16), vh) + _dot_nt(env["qs"][:, cols], st.astype(BF16))
                st_ref[h] = env["dl"][:, cols] * st + _dot_tn(vh, env["kf"][:, cols])
                gate = hg_ref[rows, cols]
                o_ref[rows, cols] = (_rms(o, gn) * (gate * jax.nn.sigmoid(gate))).astype(BF16)
            return stage

        return [prologue] + [head(h) for h in range(HG_HEADS)]

    return [stage for sub in range(n_sub) for stage in tile(sub)]


def _hgrn_consts(tt):
    idx = np.arange(tt)
    tri = jnp.asarray(idx[None, :] <= idx[:, None], BF16)
    diff = idx[:, None] ^ idx[None, :]
    level = np.where(diff > 0, np.floor(np.log2(np.maximum(diff, 1))), -1).astype(np.int32)
    lv = jnp.asarray(np.where(idx[None, :] <= idx[:, None], level, tt), jnp.int32)
    rows = np.broadcast_to(idx[:, None], (tt, LANES))
    bias = lambda keep: np.where(keep, 0.0, NEG)
    one = lambda keep: np.where(keep, 1.0, 0.0)
    cst = [bias(rows % 2 == 1), one(rows % 2 == 0),
           one(rows % 4 == 3), bias(rows % 4 >= 2), one(rows % 4 == 0), bias(rows % 4 < 2)]
    for lvl in range(2, tt.bit_length() - 1):
        upper = (rows >> lvl) % 2 == 1
        cst += [bias(upper), bias(~upper)]
    cst = jnp.asarray(np.stack(cst), F32)
    return tri, lv, cst


def _hgrn(hz, lb, gn, s0, batch, tt, n_sub=1):
    t_all = hz.shape[0]
    blk = tt * n_sub
    steps = t_all // batch // blk
    tri, lv, cst = _hgrn_consts(tt)
    col = lambda c: pl.BlockSpec((blk, HG_WIDTH), lambda b, s: (b * steps + s, c))
    shared_s0 = s0.shape[0] == 1
    state = lambda shared: pl.BlockSpec((1, HG_HEADS, HG_KEY_DIM, HG_VAL_DIM),
                                        (lambda b, s: (0, 0, 0, 0)) if shared else (lambda b, s: (b, 0, 0, 0)))
    return pl.pallas_call(
        functools.partial(_hgrn_body, tt=tt, n_sub=n_sub),
        grid=(batch, steps),
        in_specs=[col(0), col(1), col(2), col(3), _const_spec(lb.shape), _const_spec(gn.shape), state(shared_s0),
                  _const_spec(tri.shape), _const_spec(lv.shape), _const_spec(cst.shape)],
        out_specs=[pl.BlockSpec((blk, HG_WIDTH), lambda b, s: (b * steps + s, 0)), state(False)],
        out_shape=[jax.ShapeDtypeStruct((t_all, HG_WIDTH), BF16),
                   jax.ShapeDtypeStruct((batch, HG_HEADS, HG_KEY_DIM, HG_VAL_DIM), F32)],
        scratch_shapes=[pltpu.VMEM((HG_HEADS, HG_VAL_DIM, HG_KEY_DIM), F32)],
        compiler_params=_params(2),
        name="hgrn",
    )(hz, hz, hz, hz, lb, gn, s0, tri, lv, cst)


def _inproj_hgrn_body(x_ref, cosk_ref, sink_ref, cosq_ref, sinq_ref, an_ref, win_ref, qn_ref, kvn_ref, wuq_ref, w2_ref,
                      lb_ref, gn_ref, s0_ref, tri_ref, lv_ref, cst_ref,
                      ckv_ref, krope_ref, kcat_ref, qcat_ref, ohg_ref, sout_ref, hza_scr, hzb_scr, st_ref,
                      *, steps, steps_per_stream, tt, n_sub):
    s = pl.program_id(0)
    bufs = (hza_scr, hzb_scr)

    def project(dst):
        return _inproj_stages(x_ref, cosk_ref, sink_ref, cosq_ref, sinq_ref, an_ref, win_ref, qn_ref, kvn_ref,
                              wuq_ref, w2_ref, ckv_ref, krope_ref, kcat_ref, qcat_ref, dst)

    def recur(src):
        part = lambda c: src.at[:, c * HG_WIDTH:(c + 1) * HG_WIDTH]
        return _hgrn_stages(part(0), part(1), part(2), part(3), lb_ref, gn_ref, tri_ref, lv_ref, cst_ref, ohg_ref,
                            st_ref, tt=tt, n_sub=n_sub)

    def run(*stage_lists):
        lists = sorted((list(l) for l in stage_lists), key=len, reverse=True)
        total = sum(len(l) for l in lists)
        done = [0] * len(lists)
        for k in range(1, total + 1):
            for i, l in enumerate(lists):
                while done[i] < len(l) and done[i] * total < k * len(l):
                    l[done[i]]()
                    done[i] += 1

    @pl.when((s >= 1) & ((s - 1) % steps_per_stream == 0))
    def _():
        for h in range(HG_HEADS):
            st_ref[h] = s0_ref[0, h].T

    @pl.when(s == 0)
    def _():
        run(project(bufs[0]))

    for parity in (0, 1):
        @pl.when((s >= 1) & (s < steps) & (s % 2 == parity))
        def _(parity=parity):
            run(project(bufs[parity]), recur(bufs[1 - parity]))

    @pl.when(s == steps)
    def _():
        run(recur(bufs[(steps - 1) % 2]))

    @pl.when((s >= 1) & (s % steps_per_stream == 0))
    def _():
        for h in range(HG_HEADS):
            sout_ref[0, h] = st_ref[h].T


def _inproj_hgrn(x2d, tabs, wts, lb, gn, s0, batch, tm, tt):
    t = x2d.shape[0]
    steps = t // tm
    steps_per_stream = steps // batch
    period = tabs[0].shape[0] // tm
    tri, lv, cst = _hgrn_consts(tt)
    cur = lambda s: jnp.minimum(s, steps - 1)
    prev = lambda s: jnp.maximum(s - 1, 0)
    row = lambda w: pl.BlockSpec((tm, w), lambda s: (cur(s), 0))
    tab = pl.BlockSpec((tm, LANES), lambda s: (cur(s) % period, 0))
    state = lambda index_map: pl.BlockSpec((1, HG_HEADS, HG_KEY_DIM, HG_VAL_DIM), index_map)
    an, win, qn, kvn, wuq, w2 = wts
    consts = (an, win, qn, kvn, wuq, w2, lb, gn)
    return pl.pallas_call(
        functools.partial(_inproj_hgrn_body, steps=steps, steps_per_stream=steps_per_stream, tt=tt, n_sub=tm // tt),
        grid=(steps + 1,),
        in_specs=[row(D_MODEL), tab, tab, tab, tab] + [_const_spec(c.shape) for c in consts]
                 + [state(lambda s: (0, 0, 0, 0)), _const_spec(tri.shape), _const_spec(lv.shape), _const_spec(cst.shape)],
        out_specs=[row(KV_LORA_RANK), row(ROPE_DIM), row(KCAT),
                   pl.BlockSpec((MLA_HEADS, tm, KCAT), lambda s: (0, cur(s), 0)),
                   pl.BlockSpec((tm, HG_WIDTH), lambda s: (prev(s), 0)),
                   state(lambda s: (prev(s) // steps_per_stream, 0, 0, 0))],
        out_shape=[jax.ShapeDtypeStruct((t, KV_LORA_RANK), F32), jax.ShapeDtypeStruct((t, ROPE_DIM), F32),
                   jax.ShapeDtypeStruct((t, KCAT), BF16), jax.ShapeDtypeStruct((MLA_HEADS, t, KCAT), BF16),
                   jax.ShapeDtypeStruct((t, HG_WIDTH), BF16),
                   jax.ShapeDtypeStruct((batch, HG_HEADS, HG_KEY_DIM, HG_VAL_DIM), F32)],
        scratch_shapes=[pltpu.VMEM((tm, 4 * HG_WIDTH), F32), pltpu.VMEM((tm, 4 * HG_WIDTH), F32),
                        pltpu.VMEM((HG_HEADS, HG_VAL_DIM, HG_KEY_DIM), F32)],
        compiler_params=_params(1),
        name="inproj_hgrn",
    )(x2d, *tabs, *consts, s0, tri, lv, cst)


def _lanes(x, n):
    return jnp.concatenate([x] * n, axis=1)


def _softmax_update(s, kt, m_scr, acc_scr):
    m_prev = m_scr[...]
    m_new = jnp.maximum(m_prev, jnp.max(s, axis=1, keepdims=True))
    alpha = jnp.exp2(m_prev - m_new)
    p = jnp.exp2(s - _lanes(m_new, s.shape[1] // LANES))
    acc_scr[...] = _lanes(alpha, KCAT // LANES) * acc_scr[...] + _dot(p.astype(BF16), kt)
    m_scr[...] = m_new


def _attn_finish(acc, o_ref, tq, row0=0):
    o = acc[:, :KV_LORA_RANK] / acc[:, ONE_COL:ONE_COL + 1]
    for h in range(MLA_HEADS):
        o_ref[row0:row0 + tq, h * LANES:(h + 1) * LANES] = o[h * tq:(h + 1) * tq].astype(BF16)


def _attn_prompt_body(q_ref, qn_ref, kmeta_ref, k_ref, o_ref, m_scr, acc_scr, sl_scr, sm_scr, sa_scr, sb_scr,
                      *, tq, tk):
    i = pl.program_id(1)
    rows = MLA_HEADS * tq
    n_tiles = k_ref.shape[0] // tk
    q = q_ref[...].reshape(rows, KCAT)
    kmeta = kmeta_ref[...]
    n = (i * tq) // tk

    def keys(t):
        return k_ref[pl.ds(pl.multiple_of(t * tk, tk), tk), :]

    def scores(qv, t, dst):
        dst[...] = _dot_nt(qv, keys(t))

    def consume(src, t):
        _softmax_update(src[...], keys(t), m_scr, acc_scr)

    def first_stage(qv, t):
        scores(qv, t, sl_scr)
        sm_scr[...] = _dot_nt(qv, kmeta)

    def prefetch():
        first_stage(qn_ref[...].reshape(rows, KCAT), jnp.minimum(((i + 1) * tq) // tk, n_tiles - 1))

    @pl.when(i == 0)
    def _():
        first_stage(q, 0)

    scores(q, 0, sa_scr)
    limit = ((i * tq + lax.broadcasted_iota(jnp.int32, (tq, tk), 0)) // CHUNK + 1) * CHUNK - n * tk
    bias = jnp.where(lax.broadcasted_iota(jnp.int32, (tq, tk), 1) < limit, 0.0, NEG)
    s_last = sl_scr[...] + jnp.concatenate([bias] * MLA_HEADS, axis=0)
    bias_meta = jnp.where(lax.broadcasted_iota(jnp.int32, (tq, LANES), 1) < N_META, 0.0, NEG)
    s_meta = sm_scr[...] + jnp.concatenate([bias_meta] * MLA_HEADS, axis=0)
    m0 = jnp.maximum(jnp.max(s_last, axis=1, keepdims=True), jnp.max(s_meta, axis=1, keepdims=True))
    m0 = jnp.broadcast_to(m0, (rows, LANES))
    p_last = jnp.exp2(s_last - _lanes(m0, tk // LANES)).astype(BF16)
    p_meta = jnp.exp2(s_meta - m0).astype(BF16)
    acc_scr[...] = _dot(p_last, keys(n)) + _dot(p_meta, kmeta)
    m_scr[...] = m0

    bufs = (sa_scr, sb_scr)

    def run(first, count, last):
        for j in range(count):
            if j + 1 < count or not last:
                scores(q, first + j + 1, bufs[(j + 1) % 2])
            else:
                prefetch()
            consume(bufs[j % 2], first + j)

    def body(p, carry):
        run(ATTN_UNROLL * p, ATTN_UNROLL, False)
        return carry

    trips = jnp.maximum(n - 1, 0) // ATTN_UNROLL
    lax.fori_loop(0, trips, body, 0)
    base = ATTN_UNROLL * trips

    @pl.when(n == 0)
    def _():
        prefetch()

    for count in range(1, ATTN_UNROLL + 1):
        @pl.when(n - base == count)
        def _(count=count):
            run(base, count, True)

    _attn_finish(acc_scr[...], o_ref, tq)


def _attn_prompt(qcat, kmeta, kcat, batch, tq, tk):
    t_all = kcat.shape[0]
    seq = t_all // batch
    nq = seq // tq
    rows = MLA_HEADS * tq
    last_block = t_all // tq - 1
    return pl.pallas_call(
        functools.partial(_attn_prompt_body, tq=tq, tk=tk),
        grid=(batch, nq),
        in_specs=[pl.BlockSpec((MLA_HEADS, tq, KCAT), lambda b, i: (0, b * nq + i, 0)),
                  pl.BlockSpec((MLA_HEADS, tq, KCAT), lambda b, i: (0, jnp.minimum(b * nq + i + 1, last_block), 0)),
                  _const_spec(kmeta.shape),
                  pl.BlockSpec((seq, KCAT), lambda b, i: (b, 0))],
        out_specs=pl.BlockSpec((tq, MLA_HEADS * LANES), lambda b, i: (b * nq + i, 0)),
        out_shape=jax.ShapeDtypeStruct((t_all, MLA_HEADS * LANES), BF16),
        scratch_shapes=[pltpu.VMEM((rows, LANES), F32), pltpu.VMEM((rows, KCAT), F32),
                        pltpu.VMEM((rows, tk), F32), pltpu.VMEM((rows, LANES), F32),
                        pltpu.VMEM((rows, tk), F32), pltpu.VMEM((rows, tk), F32)],
        compiler_params=_params(2),
        name="attn_prompt",
    )(qcat, qcat, kmeta, kcat)


def _attn_sample_body(q_ref, kc_ref, kn_ref, o_ref, *, tq, ns):
    rows = MLA_HEADS * tq
    for s in range(ns):
        q = q_ref[:, s * tq:(s + 1) * tq, :].reshape(rows, KCAT)
        kc = kc_ref[s]
        kn = kn_ref[s * tq:(s + 1) * tq, :]
        s_c = _dot_nt(q, kc)
        s_n = _dot_nt(q, kn)
        m = jnp.maximum(jnp.max(s_c, axis=1, keepdims=True), jnp.max(s_n, axis=1, keepdims=True))
        acc = _dot(jnp.exp2(s_c - m).astype(BF16), kc) + _dot(jnp.exp2(s_n - m).astype(BF16), kn)
        _attn_finish(acc, o_ref, tq, s * tq)


def _attn_sample(qcat, kcache, kcat, tq, ns):
    batch = kcache.shape[0]
    t_all = kcat.shape[0]
    return pl.pallas_call(
        functools.partial(_attn_sample_body, tq=tq, ns=ns),
        grid=(batch // ns,),
        in_specs=[pl.BlockSpec((MLA_HEADS, ns * tq, KCAT), lambda b: (0, b, 0)),
                  pl.BlockSpec((ns,) + kcache.shape[1:], lambda b: (b, 0, 0)),
                  pl.BlockSpec((ns * tq, KCAT), lambda b: (b, 0))],
        out_specs=pl.BlockSpec((ns * tq, MLA_HEADS * LANES), lambda b: (b, 0)),
        out_shape=jax.ShapeDtypeStruct((t_all, MLA_HEADS * LANES), BF16),
        compiler_params=_params(1),
        name="attn_sample",
    )(qcat, kcache, kcat)


def _outproj_body(x_ref, olat_ref, ohg_ref, wuv_ref, wo_ref, fn_ref, wr_ref, br_ref,
                  h1_ref, hn_ref, gate_ref, *, n_split):
    tm = x_ref.shape[0] // n_split
    for r in range(n_split):
        rows = slice(r * tm, (r + 1) * tm)
        _outproj_rows(x_ref.at[rows], olat_ref.at[rows], ohg_ref.at[rows], wuv_ref, wo_ref, fn_ref, wr_ref,
                      br_ref, h1_ref.at[rows], hn_ref.at[rows], gate_ref.at[rows])


def _outproj_rows(x_ref, olat_ref, ohg_ref, wuv_ref, wo_ref, fn_ref, wr_ref, br_ref, h1_ref, hn_ref, gate_ref):
    o_mla = _dot(olat_ref[...], wuv_ref[...]).astype(BF16)
    mix = _dot(o_mla, wo_ref[:MLA_WIDTH, :]) + _dot(ohg_ref[...], wo_ref[MLA_WIDTH:, :])
    h1 = x_ref[...] + mix
    h1_ref[...] = h1
    hn = _rms(h1, fn_ref[...])
    hn_hi = hn.astype(BF16)
    hn_ref[...] = hn_hi
    hn_lo = (hn - hn_hi.astype(F32)).astype(BF16)
    r2 = _dot(jnp.concatenate([hn_hi, hn_lo], axis=1), wr_ref[...])
    logits = r2[:, :LANES] + r2[:, LANES:] + br_ref[...]
    lane = lax.broadcasted_iota(jnp.int32, logits.shape, 1)
    gl = jnp.where(lane < N_GROUPS, logits, NEG)
    gmax = jnp.max(gl, axis=1, keepdims=True)
    g_top = 1.0 / jnp.sum(jnp.exp(gl - gmax), axis=1, keepdims=True)
    g_idx = jnp.min(jnp.where(gl == gmax, lane, LANES), axis=1, keepdims=True)
    lo = GATE_LANE0 + EXPERTS_PER_GROUP * g_idx
    el = jnp.where((lane >= lo) & (lane < lo + EXPERTS_PER_GROUP), logits, NEG)
    e1 = jnp.max(el, axis=1, keepdims=True)
    i1 = jnp.min(jnp.where(el == e1, lane, LANES), axis=1, keepdims=True)
    el2 = jnp.where(lane == i1, NEG, el)
    e2 = jnp.max(el2, axis=1, keepdims=True)
    i2 = jnp.min(jnp.where(el2 == e2, lane, LANES), axis=1, keepdims=True)
    esum = jnp.sum(jnp.exp(el - e1), axis=1, keepdims=True)
    p1 = 1.0 / esum
    p2 = jnp.exp(e2 - e1) / esum
    w1 = p1 / (p1 + p2) * g_top
    w2 = p2 / (p1 + p2) * g_top
    gate_ref[...] = jnp.where(lane == i1, w1, jnp.where(lane == i2, w2, 0.0))


def _outproj(x2d, olat, ohg, wts, tm):
    t = x2d.shape[0]
    row = lambda w: pl.BlockSpec((tm, w), lambda i: (i, 0))
    return pl.pallas_call(
        functools.partial(_outproj_body, n_split=max(1, tm // 256)),
        grid=(t // tm,),
        in_specs=[row(D_MODEL), row(MLA_HEADS * LANES), row(HG_WIDTH)] + [_const_spec(w.shape) for w in wts],
        out_specs=[row(D_MODEL), row(D_MODEL), row(LANES)],
        out_shape=[jax.ShapeDtypeStruct((t, D_MODEL), F32), jax.ShapeDtypeStruct((t, D_MODEL), BF16),
                   jax.ShapeDtypeStruct((t, LANES), F32)],
        compiler_params=_params(1),
        name="outproj",
    )(x2d, olat, ohg, *wts)


def _moe_body(hn_ref, h1_ref, gate_ref, w1_ref, w3_ref, w2_ref, fin_ref, y_ref):
    t = hn_ref[...]
    gates = gate_ref[...]
    acc = jnp.zeros(h1_ref.shape, F32)
    for grp in range(N_GROUPS):
        parts = []
        for j in range(EXPERTS_PER_GROUP):
            e = grp * EXPERTS_PER_GROUP + j
            a = _dot(t, w1_ref[e])
            c = _dot(t, w3_ref[e])
            lane = GATE_LANE0 + e
            parts.append((a * jax.nn.sigmoid(a) * c * gates[:, lane:lane + 1]).astype(BF16))
        acc = acc + _dot(jnp.concatenate(parts, axis=-1), w2_ref[grp])
    y_ref[...] = _rms(h1_ref[...] + acc, fin_ref[...])


def _moe(hn, h1, gates, w1g, w3g, w2g, fin, tm):
    t = hn.shape[0]
    row = lambda w: pl.BlockSpec((tm, w), lambda i: (i, 0))
    resident = lambda a: pl.BlockSpec(a.shape, lambda i: (0,) * a.ndim, pipeline_mode=pl.Buffered(1))
    return pl.pallas_call(
        _moe_body,
        grid=(t // tm,),
        in_specs=[row(D_MODEL), row(D_MODEL), row(LANES), resident(w1g), resident(w3g), resident(w2g),
                  _const_spec(fin.shape)],
        out_specs=row(D_MODEL),
        out_shape=jax.ShapeDtypeStruct((t, D_MODEL), F32),
        compiler_params=_params(1),
        name="moe",
    )(hn, h1, gates, w1g, w3g, w2g, fin)


def _rope_tables(pos):
    half = ROPE_DIM // 2
    inv = ROPE_THETA ** (-jnp.arange(half, dtype=F32) / half)
    ang = pos.astype(F32)[:, None] * inv[None, :]
    cos, sin = jnp.cos(ang), jnp.sin(ang)
    n = pos.shape[0]
    z = lambda w: jnp.zeros((n, w), F32)
    cosk = jnp.concatenate([cos, cos, z(LANES - ROPE_DIM)], axis=1)
    sink = jnp.concatenate([-sin, sin, z(LANES - ROPE_DIM)], axis=1)
    cosq = jnp.concatenate([jnp.ones((n, QK_NOPE_DIM), F32), cos, cos, z(ROPE_DIM)], axis=1)
    sinq = jnp.concatenate([z(QK_NOPE_DIM), -sin, sin, z(ROPE_DIM)], axis=1)
    return cosk, sink, cosq, sinq


def kernel(x_prompt, x_sample, cache_ckv, cache_krope, state_hgrn, meta_tokens, attn_norm, w_in, q_norm, kv_norm, w_uq, w_uk, w_uv, hg_lb_logits, hg_gnorm, w_o, ffn_norm, w_router_group, b_router_group, w_router_expert, b_router_expert, w1, w3, w2, final_norm):
    bp, seq, _ = x_prompt.shape
    bs, dseq, _ = x_sample.shape
    half = ROPE_DIM // 2
    swap = np.concatenate([np.arange(half, ROPE_DIM), np.arange(half)])

    lb = jax.nn.softmax(hg_lb_logits.astype(F32), axis=0)[0][None, :]
    wi = w_in[0]
    o_kv, o_kr, o_hg = Q_LORA_RANK, Q_LORA_RANK + KV_LORA_RANK, Q_LORA_RANK + KV_LORA_RANK + ROPE_DIM
    w_kr = wi[:, o_kr:o_hg]
    win_ext = jnp.concatenate([wi[:, :o_kr], w_kr, w_kr[:, swap], jnp.zeros((D_MODEL, LANES - 2 * ROPE_DIM), F32),
                               wi[:, o_hg:]], axis=1).astype(BF16)
    wq = w_uq[0]
    wq_rope = wq[:, :, QK_NOPE_DIM:]
    wuq_ext = jnp.concatenate([wq, wq_rope[:, :, swap]], axis=2).reshape(Q_LORA_RANK, MLA_HEADS * LANES).astype(BF16)
    eye = jnp.eye(ROPE_DIM, dtype=F32)
    w2q = jnp.zeros((MLA_HEADS, LANES, KCAT), F32)
    w2q = w2q.at[:, :QK_NOPE_DIM, :KV_LORA_RANK].set(jnp.transpose(w_uk[0], (1, 2, 0)))
    w2q = w2q.at[:, QK_NOPE_DIM:QK_NOPE_DIM + ROPE_DIM, KV_LORA_RANK:KV_LORA_RANK + ROPE_DIM].set(eye[None])
    w2q = w2q.astype(BF16)
    wuv_bd = jnp.zeros((MLA_HEADS, KV_LORA_RANK, MLA_HEADS, V_HEAD_DIM), F32)
    for h in range(MLA_HEADS):
        wuv_bd = wuv_bd.at[h, :, h, :].set(w_uv[0][:, h, :])
    wuv_bd = wuv_bd.reshape(MLA_HEADS * KV_LORA_RANK, MLA_WIDTH).astype(BF16)
    wo = w_o[0].astype(BF16)
    wr = jnp.concatenate([w_router_group[0], w_router_expert[0],
                          jnp.zeros((D_MODEL, LANES - N_GROUPS - N_EXPERTS), F32)], axis=1)
    wr_hi = wr.astype(BF16)
    wr_lo = (wr - wr_hi.astype(F32)).astype(BF16)
    wr2 = jnp.concatenate([jnp.concatenate([wr_hi, wr_lo], axis=1),
                           jnp.concatenate([wr_hi, jnp.zeros_like(wr_lo)], axis=1)], axis=0)
    br = jnp.concatenate([b_router_group[0], b_router_expert[0],
                          jnp.zeros((LANES - N_GROUPS - N_EXPERTS,), F32)])[None, :].astype(F32)
    w1g = w1[0].astype(BF16)
    w3g = w3[0].astype(BF16)
    w2g = w2[0].astype(BF16).reshape(N_GROUPS, EXPERTS_PER_GROUP * EXPERT_FF, D_MODEL)
    in_wts = (attn_norm[0][None, :], win_ext, q_norm[0][None, :], kv_norm[0][None, :], wuq_ext, w2q)
    out_wts = (wuv_bd, wo, ffn_norm[0][None, :], wr2, br)
    gn = hg_gnorm[0][None, :]
    fin = final_norm[None, :]

    tm_s = 256
    tab_meta = _rope_tables(jnp.arange(N_META))
    tab_p = _rope_tables(N_META + jnp.arange(seq))
    tab_s = tuple(jnp.tile(a, (tm_s // dseq, 1)) for a in _rope_tables(PAST_LEN + jnp.arange(dseq)))
    ckv_m, kr_m, kcat_m, _, hz_m = _inproj(meta_tokens, tab_meta, in_wts, N_META)
    xp = x_prompt.reshape(bp * seq, D_MODEL)
    xs = x_sample.reshape(bs * dseq, D_MODEL)
    ckv_s, kr_s, kcat_s, qcat_s, hz_s = _inproj(xs, tab_s, in_wts, tm_s)

    zero_state = jnp.zeros((1, HG_HEADS, HG_KEY_DIM, HG_VAL_DIM), F32)
    _, st_m = _hgrn(hz_m, lb, gn, zero_state, 1, N_META)
    ckv_p, kr_p, kcat_p, qcat_p, ohg_p, st_p = _inproj_hgrn(xp, tab_p, in_wts, lb, gn, st_m, bp, 512, 128)
    ohg_s, st_s = _hgrn(hz_s, lb, gn, state_hgrn[0].astype(F32), bs, dseq)

    kmeta = jnp.concatenate([kcat_m, jnp.zeros((LANES - N_META, KCAT), BF16)], axis=0)
    olat_p = _attn_prompt(qcat_p, kmeta, kcat_p, bp, 128, 512)
    n_c = cache_ckv.shape[2]
    kcache = jnp.concatenate([cache_ckv[0], cache_krope[0], jnp.ones((bs, n_c, 1), F32),
                              jnp.zeros((bs, n_c, KCAT - ONE_COL - 1), F32)], axis=-1).astype(BF16)
    olat_s = _attn_sample(qcat_s, kcache, kcat_s, dseq, 4)

    h1_p, hn_p, gate_p = _outproj(xp, olat_p, ohg_p, out_wts, 1024)
    h1_s, hn_s, gate_s = _outproj(xs, olat_s, ohg_s, out_wts, 256)
    y_p = _moe(hn_p, h1_p, gate_p, w1g, w3g, w2g, fin, 512)
    y_s = _moe(hn_s, h1_s, gate_s, w1g, w3g, w2g, fin, 256)

    with_meta = lambda m, a, w: jnp.concatenate(
        [jnp.broadcast_to(m[None], (bp, N_META, w)), a.reshape(bp, seq, w)], axis=1)[None]
    return (y_p.reshape(bp, seq, D_MODEL), y_s.reshape(bs, dseq, D_MODEL),
            with_meta(ckv_m, ckv_p, KV_LORA_RANK), with_meta(kr_m, kr_p, ROPE_DIM), st_p[None],
            ckv_s.reshape(1, bs, dseq, KV_LORA_RANK), kr_s.reshape(1, bs, dseq, ROPE_DIM), st_s[None])
```

```python
import functools

import numpy as np
import jax
import jax.numpy as jnp
from jax import lax
from jax.experimental import pallas as pl
from jax.experimental.pallas import tpu as pltpu

F32 = jnp.float32
BF16 = jnp.bfloat16

D_MODEL = 1024
PAST_LEN = 2048
CHUNK = 64
N_META = 16
EPS = 1e-6
MLA_HEADS = 8
Q_LORA_RANK = 256
KV_LORA_RANK = 128
QK_NOPE_DIM = 64
ROPE_DIM = 32
V_HEAD_DIM = 64
ROPE_THETA = 10000.0
MLA_WIDTH = MLA_HEADS * V_HEAD_DIM
SM_SCALE = (QK_NOPE_DIM + ROPE_DIM) ** -0.5
LOG2E = 1.4426950408889634
Q_SCALE = SM_SCALE * LOG2E
HG_HEADS = 4
HG_KEY_DIM = 128
HG_VAL_DIM = 128
HG_WIDTH = HG_HEADS * HG_VAL_DIM
N_GROUPS = 4
EXPERTS_PER_GROUP = 4
N_EXPERTS = N_GROUPS * EXPERTS_PER_GROUP
EXPERT_FF = 256

LANES = 128
KCAT = 2 * LANES
ONE_COL = LANES + ROPE_DIM
NEG = -1e30
VMEM_LIMIT = 56 * 1024 * 1024
GATE_LANE0 = N_GROUPS
ATTN_UNROLL = 4

TILE_INPROJ = (512, 256)
TILE_HGRN = 128
HGRN_TILES_PER_STEP = 8
ATTN_TQ, ATTN_TK = 128, 512
ATTN_SAMPLE_STREAMS = 4
TILE_OUTPROJ = (1024, 256)
OUTPROJ_SLICE = 256
TILE_MOE = (512, 256)


def _rms(x, w):
    return x * lax.rsqrt(jnp.mean(x * x, axis=-1, keepdims=True) + EPS) * w


def _dot(a, b):
    return jnp.dot(a, b, preferred_element_type=F32)


def _dot_nt(a, b):
    return lax.dot_general(a, b, (((1,), (1,)), ((), ())), preferred_element_type=F32)


def _dot_tn(a, b):
    return lax.dot_general(a, b, (((0,), (0,)), ((), ())), preferred_element_type=F32)


def _split3(x):
    x1 = x.astype(BF16)
    r = x - x1.astype(F32)
    x2 = r.astype(BF16)
    x3 = (r - x2.astype(F32)).astype(BF16)
    return x1, x2, x3


def _params(n_axes):
    return pltpu.CompilerParams(dimension_semantics=("arbitrary",) * n_axes, vmem_limit_bytes=VMEM_LIMIT)


def _const_spec(shape):
    nd = len(shape)
    return pl.BlockSpec(shape, lambda *_: (0,) * nd)


def _inproj_body(x_ref, cosk_ref, sink_ref, cosq_ref, sinq_ref, an_ref, win_ref, qn_ref, kvn_ref, wuq_ref, w2_ref,
                 ckv_ref, krope_ref, kcat_ref, qcat_ref, hz_ref):
    x = x_ref[...]
    hn = _rms(x, an_ref[...]).astype(BF16)
    z = _dot(hn, win_ref[...])
    hz_ref[...] = z[:, 4 * LANES:]
    ckv = _rms(z[:, 2 * LANES:3 * LANES], kvn_ref[...])
    ckv_ref[...] = ckv
    kr = z[:, 3 * LANES:4 * LANES]
    rot = kr * cosk_ref[...] + pltpu.roll(kr, LANES - ROPE_DIM, 1) * sink_ref[...]
    krope_ref[...] = rot[:, :ROPE_DIM]
    lane = lax.broadcasted_iota(jnp.int32, rot.shape, 1)
    kcat_ref[...] = jnp.concatenate([ckv, jnp.where(lane == ROPE_DIM, 1.0, rot)], axis=-1).astype(BF16)
    cqn = _rms(z[:, :Q_LORA_RANK], qn_ref[...]).astype(BF16)
    qz = _dot(cqn, wuq_ref[...])
    cosq = cosq_ref[...]
    sinq = sinq_ref[...]
    for h in range(MLA_HEADS):
        blk = qz[:, h * LANES:(h + 1) * LANES]
        r = blk * cosq + pltpu.roll(blk, LANES - ROPE_DIM, 1) * sinq
        qcat_ref[h] = (_dot(r.astype(BF16), w2_ref[h]) * Q_SCALE).astype(BF16)


def _inproj(x2d, tabs, wts, tm):
    t = x2d.shape[0]
    period = tabs[0].shape[0] // tm
    grid = (t // tm,)
    row = lambda w: pl.BlockSpec((tm, w), lambda i: (i, 0))
    tab = pl.BlockSpec((tm, LANES), lambda i: (i % period, 0))
    an, win, qn, kvn, wuq, w2 = wts
    return pl.pallas_call(
        _inproj_body,
        grid=grid,
        in_specs=[row(D_MODEL), tab, tab, tab, tab, _const_spec(an.shape), _const_spec(win.shape),
                  _const_spec(qn.shape), _const_spec(kvn.shape), _const_spec(wuq.shape), _const_spec(w2.shape)],
        out_specs=[row(KV_LORA_RANK), row(ROPE_DIM), row(KCAT),
                   pl.BlockSpec((MLA_HEADS, tm, KCAT), lambda i: (0, i, 0)), row(4 * HG_WIDTH)],
        out_shape=[jax.ShapeDtypeStruct((t, KV_LORA_RANK), F32), jax.ShapeDtypeStruct((t, ROPE_DIM), F32),
                   jax.ShapeDtypeStruct((t, KCAT), BF16), jax.ShapeDtypeStruct((MLA_HEADS, t, KCAT), BF16),
                   jax.ShapeDtypeStruct((t, 4 * HG_WIDTH), F32)],
        compiler_params=_params(1),
        name="inproj",
    )(x2d, *tabs, an, win, qn, kvn, wuq, w2)


def _hgrn_body(hq_ref, hf_ref, hi_ref, hg_ref, lb_ref, gn_ref, s0_ref, tri_ref, lv_ref, cst_ref,
               o_ref, sout_ref, st_ref, *, tt, n_sub):
    step = pl.program_id(1)
    n_levels = tt.bit_length() - 1

    @pl.when(step == 0)
    def _():
        for h in range(HG_HEADS):
            st_ref[h] = s0_ref[0, h].T

    lb = lb_ref[...]
    gn = gn_ref[...]
    tri = tri_ref[...]
    lv = lv_ref[...]
    for sub in range(n_sub):
        rows = slice(sub * tt, (sub + 1) * tt)
        hq = hq_ref[rows, :]
        q = hq * jax.nn.sigmoid(hq) * (HG_KEY_DIM ** -0.5)
        f = lb + (1.0 - lb) * jax.nn.sigmoid(hf_ref[rows, :])
        g = jnp.log(f) * LOG2E
        kk = 1.0 - f
        v = hi_ref[rows, :]
        g1, g2, g3 = _split3(g)
        bc = _dot(tri, g1) + _dot(tri, g2) + _dot(tri, g3)

        bl = bc[tt - 1:tt, :]
        qs = (q * jnp.exp2(bc)).astype(BF16)
        kf = (kk * jnp.exp2(bl - bc)).astype(BF16)
        vb = v.astype(BF16)
        dl = jnp.exp2(bl)
        hg = hg_ref[rows, :]
        for h in range(HG_HEADS):
            cols = slice(h * LANES, (h + 1) * LANES)
            qh, kh, gh, bch = q[:, cols], kk[:, cols], g[:, cols], bc[:, cols]
            pairs = [(qh, kh)]
            pairs.append((qh * jnp.exp2(gh + cst_ref[0]), kh * cst_ref[1]))
            if n_levels > 1:
                g_prev = pltpu.roll(gh, 1, 0)
                g_next = pltpu.roll(gh, tt - 1, 0)
                pairs.append((qh * jnp.exp2(gh + g_prev * cst_ref[2] + cst_ref[3]),
                              kh * jnp.exp2(g_next * cst_ref[4] + cst_ref[5])))
            for lvl in range(2, n_levels):
                grp = 2 << lvl
                mid = bch.reshape(tt // grp, grp, LANES)[:, grp // 2 - 1:grp // 2, :]
                d = bch - jnp.broadcast_to(mid, (tt // grp, grp, LANES)).reshape(tt, LANES)
                pairs.append((qh * jnp.exp2(d + cst_ref[2 + 2 * lvl]), kh * jnp.exp2(cst_ref[3 + 2 * lvl] - d)))
            att = jnp.zeros((tt, tt), F32)
            for idx, (qa, ka) in enumerate(pairs):
                att = jnp.where(lv == idx - 1, _dot_nt(qa.astype(BF16), ka.astype(BF16)), att)
            st = st_ref[h]
            o = _dot(att.astype(BF16), vb[:, cols]) + _dot_nt(qs[:, cols], st.astype(BF16))
            st_ref[h] = dl[:, cols] * st + _dot_tn(vb[:, cols], kf[:, cols])
            gate = hg[:, cols]
            o_ref[rows, cols] = (_rms(o, gn) * (gate * jax.nn.sigmoid(gate))).astype(BF16)

    @pl.when(step == pl.num_programs(1) - 1)
    def _():
        for h in range(HG_HEADS):
            sout_ref[0, h] = st_ref[h].T


def _hgrn(hz, lb, gn, s0, batch, tt, n_sub=1):
    t_all = hz.shape[0]
    blk = tt * n_sub
    steps = t_all // batch // blk
    idx = np.arange(tt)
    tri = jnp.asarray(idx[None, :] <= idx[:, None], BF16)
    diff = idx[:, None] ^ idx[None, :]
    level = np.where(diff > 0, np.floor(np.log2(np.maximum(diff, 1))), -1).astype(np.int32)
    lv = jnp.asarray(np.where(idx[None, :] <= idx[:, None], level, tt), jnp.int32)
    rows = np.broadcast_to(idx[:, None], (tt, LANES))
    bias = lambda keep: np.where(keep, 0.0, NEG)
    one = lambda keep: np.where(keep, 1.0, 0.0)
    cst = [bias(rows % 2 == 1), one(rows % 2 == 0),
           one(rows % 4 == 3), bias(rows % 4 >= 2), one(rows % 4 == 0), bias(rows % 4 < 2)]
    for lvl in range(2, tt.bit_length() - 1):
        upper = (rows >> lvl) % 2 == 1
        cst += [bias(upper), bias(~upper)]
    cst = jnp.asarray(np.stack(cst), F32)
    col = lambda c: pl.BlockSpec((blk, HG_WIDTH), lambda b, s: (b * steps + s, c))
    shared_s0 = s0.shape[0] == 1
    state = lambda shared: pl.BlockSpec((1, HG_HEADS, HG_KEY_DIM, HG_VAL_DIM),
                                        (lambda b, s: (0, 0, 0, 0)) if shared else (lambda b, s: (b, 0, 0, 0)))
    return pl.pallas_call(
        functools.partial(_hgrn_body, tt=tt, n_sub=n_sub),
        grid=(batch, steps),
        in_specs=[col(0), col(1), col(2), col(3), _const_spec(lb.shape), _const_spec(gn.shape), state(shared_s0),
                  _const_spec(tri.shape), _const_spec(lv.shape), _const_spec(cst.shape)],
        out_specs=[pl.BlockSpec((blk, HG_WIDTH), lambda b, s: (b * steps + s, 0)), state(False)],
        out_shape=[jax.ShapeDtypeStruct((t_all, HG_WIDTH), BF16),
                   jax.ShapeDtypeStruct((batch, HG_HEADS, HG_KEY_DIM, HG_VAL_DIM), F32)],
        scratch_shapes=[pltpu.VMEM((HG_HEADS, HG_VAL_DIM, HG_KEY_DIM), F32)],
        compiler_params=_params(2),
        name="hgrn",
    )(hz, hz, hz, hz, lb, gn, s0, tri, lv, cst)


def _lanes(x, n):
    return jnp.concatenate([x] * n, axis=1)


def _softmax_update(s, kt, m_scr, acc_scr):
    m_prev = m_scr[...]
    m_new = jnp.maximum(m_prev, jnp.max(s, axis=1, keepdims=True))
    alpha = jnp.exp2(m_prev - m_new)
    p = jnp.exp2(s - _lanes(m_new, s.shape[1] // LANES))
    acc_scr[...] = _lanes(alpha, KCAT // LANES) * acc_scr[...] + _dot(p.astype(BF16), kt)
    m_scr[...] = m_new


def _attn_finish(acc, o_ref, tq, row0=0):
    o = acc[:, :KV_LORA_RANK] / acc[:, ONE_COL:ONE_COL + 1]
    for h in range(MLA_HEADS):
        o_ref[row0:row0 + tq, h * LANES:(h + 1) * LANES] = o[h * tq:(h + 1) * tq].astype(BF16)


def _attn_prompt_body(q_ref, qn_ref, kmeta_ref, k_ref, o_ref, m_scr, acc_scr, sl_scr, sm_scr, sa_scr, sb_scr,
                      *, tq, tk):
    i = pl.program_id(1)
    rows = MLA_HEADS * tq
    n_tiles = k_ref.shape[0] // tk
    q = q_ref[...].reshape(rows, KCAT)
    kmeta = kmeta_ref[...]
    n = (i * tq) // tk

    def keys(t):
        return k_ref[pl.ds(pl.multiple_of(t * tk, tk), tk), :]

    def scores(qv, t, dst):
        dst[...] = _dot_nt(qv, keys(t))

    def consume(src, t):
        _softmax_update(src[...], keys(t), m_scr, acc_scr)

    def first_stage(qv, t):
        scores(qv, t, sl_scr)
        sm_scr[...] = _dot_nt(qv, kmeta)

    def prefetch():
        first_stage(qn_ref[...].reshape(rows, KCAT), jnp.minimum(((i + 1) * tq) // tk, n_tiles - 1))

    @pl.when(i == 0)
    def _():
        first_stage(q, 0)

    scores(q, 0, sa_scr)
    limit = ((i * tq + lax.broadcasted_iota(jnp.int32, (tq, tk), 0)) // CHUNK + 1) * CHUNK - n * tk
    bias = jnp.where(lax.broadcasted_iota(jnp.int32, (tq, tk), 1) < limit, 0.0, NEG)
    s_last = sl_scr[...] + jnp.concatenate([bias] * MLA_HEADS, axis=0)
    bias_meta = jnp.where(lax.broadcasted_iota(jnp.int32, (tq, LANES), 1) < N_META, 0.0, NEG)
    s_meta = sm_scr[...] + jnp.concatenate([bias_meta] * MLA_HEADS, axis=0)
    m0 = jnp.maximum(jnp.max(s_last, axis=1, keepdims=True), jnp.max(s_meta, axis=1, keepdims=True))
    m0 = jnp.broadcast_to(m0, (rows, LANES))
    p_last = jnp.exp2(s_last - _lanes(m0, tk // LANES)).astype(BF16)
    p_meta = jnp.exp2(s_meta - m0).astype(BF16)
    acc_scr[...] = _dot(p_last, keys(n)) + _dot(p_meta, kmeta)
    m_scr[...] = m0

    bufs = (sa_scr, sb_scr)

    def run(first, count, last):
        for j in range(count):
            if j + 1 < count or not last:
                scores(q, first + j + 1, bufs[(j + 1) % 2])
            else:
                prefetch()
            consume(bufs[j % 2], first + j)

    def body(p, carry):
        run(ATTN_UNROLL * p, ATTN_UNROLL, False)
        return carry

    trips = jnp.maximum(n - 1, 0) // ATTN_UNROLL
    lax.fori_loop(0, trips, body, 0)
    base = ATTN_UNROLL * trips

    @pl.when(n == 0)
    def _():
        prefetch()

    for count in range(1, ATTN_UNROLL + 1):
        @pl.when(n - base == count)
        def _(count=count):
            run(base, count, True)

    _attn_finish(acc_scr[...], o_ref, tq)


def _attn_prompt(qcat, kmeta, kcat, batch, tq, tk):
    t_all = kcat.shape[0]
    seq = t_all // batch
    nq = seq // tq
    rows = MLA_HEADS * tq
    last_block = t_all // tq - 1
    return pl.pallas_call(
        functools.partial(_attn_prompt_body, tq=tq, tk=tk),
        grid=(batch, nq),
        in_specs=[pl.BlockSpec((MLA_HEADS, tq, KCAT), lambda b, i: (0, b * nq + i, 0)),
                  pl.BlockSpec((MLA_HEADS, tq, KCAT), lambda b, i: (0, jnp.minimum(b * nq + i + 1, last_block), 0)),
                  _const_spec(kmeta.shape),
                  pl.BlockSpec((seq, KCAT), lambda b, i: (b, 0))],
        out_specs=pl.BlockSpec((tq, MLA_HEADS * LANES), lambda b, i: (b * nq + i, 0)),
        out_shape=jax.ShapeDtypeStruct((t_all, MLA_HEADS * LANES), BF16),
        scratch_shapes=[pltpu.VMEM((rows, LANES), F32), pltpu.VMEM((rows, KCAT), F32),
                        pltpu.VMEM((rows, tk), F32), pltpu.VMEM((rows, LANES), F32),
                        pltpu.VMEM((rows, tk), F32), pltpu.VMEM((rows, tk), F32)],
        compiler_params=_params(2),
        name="attn_prompt",
    )(qcat, qcat, kmeta, kcat)


def _attn_sample_body(q_ref, kc_ref, kn_ref, o_ref, *, tq, ns):
    rows = MLA_HEADS * tq
    for s in range(ns):
        q = q_ref[:, s * tq:(s + 1) * tq, :].reshape(rows, KCAT)
        kc = kc_ref[s]
        kn = kn_ref[s * tq:(s + 1) * tq, :]
        s_c = _dot_nt(q, kc)
        s_n = _dot_nt(q, kn)
        m = jnp.maximum(jnp.max(s_c, axis=1, keepdims=True), jnp.max(s_n, axis=1, keepdims=True))
        acc = _dot(jnp.exp2(s_c - m).astype(BF16), kc) + _dot(jnp.exp2(s_n - m).astype(BF16), kn)
        _attn_finish(acc, o_ref, tq, s * tq)


def _attn_sample(qcat, kcache, kcat, tq, ns):
    batch = kcache.shape[0]
    t_all = kcat.shape[0]
    return pl.pallas_call(
        functools.partial(_attn_sample_body, tq=tq, ns=ns),
        grid=(batch // ns,),
        in_specs=[pl.BlockSpec((MLA_HEADS, ns * tq, KCAT), lambda b: (0, b, 0)),
                  pl.BlockSpec((ns,) + kcache.shape[1:], lambda b: (b, 0, 0)),
                  pl.BlockSpec((ns * tq, KCAT), lambda b: (b, 0))],
        out_specs=pl.BlockSpec((ns * tq, MLA_HEADS * LANES), lambda b: (b, 0)),
        out_shape=jax.ShapeDtypeStruct((t_all, MLA_HEADS * LANES), BF16),
        compiler_params=_params(1),
        name="attn_sample",
    )(qcat, kcache, kcat)


def _outproj_body(x_ref, olat_ref, ohg_ref, wuv_ref, wo_ref, fn_ref, wr_ref, br_ref,
                  h1_ref, hn_ref, gate_ref, *, n_split):
    tm = x_ref.shape[0] // n_split
    for r in range(n_split):
        rows = slice(r * tm, (r + 1) * tm)
        _outproj_rows(x_ref.at[rows], olat_ref.at[rows], ohg_ref.at[rows], wuv_ref, wo_ref, fn_ref, wr_ref,
                      br_ref, h1_ref.at[rows], hn_ref.at[rows], gate_ref.at[rows])


def _outproj_rows(x_ref, olat_ref, ohg_ref, wuv_ref, wo_ref, fn_ref, wr_ref, br_ref, h1_ref, hn_ref, gate_ref):
    o_mla = _dot(olat_ref[...], wuv_ref[...]).astype(BF16)
    mix = _dot(o_mla, wo_ref[:MLA_WIDTH, :]) + _dot(ohg_ref[...], wo_ref[MLA_WIDTH:, :])
    h1 = x_ref[...] + mix
    h1_ref[...] = h1
    hn = _rms(h1, fn_ref[...])
    hn_hi = hn.astype(BF16)
    hn_ref[...] = hn_hi
    hn_lo = (hn - hn_hi.astype(F32)).astype(BF16)
    r2 = _dot(jnp.concatenate([hn_hi, hn_lo], axis=1), wr_ref[...])
    logits = r2[:, :LANES] + r2[:, LANES:] + br_ref[...]
    lane = lax.broadcasted_iota(jnp.int32, logits.shape, 1)
    gl = jnp.where(lane < N_GROUPS, logits, NEG)
    gmax = jnp.max(gl, axis=1, keepdims=True)
    g_top = 1.0 / jnp.sum(jnp.exp(gl - gmax), axis=1, keepdims=True)
    g_idx = jnp.min(jnp.where(gl == gmax, lane, LANES), axis=1, keepdims=True)
    lo = GATE_LANE0 + EXPERTS_PER_GROUP * g_idx
    el = jnp.where((lane >= lo) & (lane < lo + EXPERTS_PER_GROUP), logits, NEG)
    e1 = jnp.max(el, axis=1, keepdims=True)
    i1 = jnp.min(jnp.where(el == e1, lane, LANES), axis=1, keepdims=True)
    el2 = jnp.where(lane == i1, NEG, el)
    e2 = jnp.max(el2, axis=1, keepdims=True)
    i2 = jnp.min(jnp.where(el2 == e2, lane, LANES), axis=1, keepdims=True)
    esum = jnp.sum(jnp.exp(el - e1), axis=1, keepdims=True)
    p1 = 1.0 / esum
    p2 = jnp.exp(e2 - e1) / esum
    w1 = p1 / (p1 + p2) * g_top
    w2 = p2 / (p1 + p2) * g_top
    gate_ref[...] = jnp.where(lane == i1, w1, jnp.where(lane == i2, w2, 0.0))


def _outproj(x2d, olat, ohg, wts, tm):
    t = x2d.shape[0]
    row = lambda w: pl.BlockSpec((tm, w), lambda i: (i, 0))
    return pl.pallas_call(
        functools.partial(_outproj_body, n_split=max(1, tm // OUTPROJ_SLICE)),
        grid=(t // tm,),
        in_specs=[row(D_MODEL), row(MLA_HEADS * LANES), row(HG_WIDTH)] + [_const_spec(w.shape) for w in wts],
        out_specs=[row(D_MODEL), row(D_MODEL), row(LANES)],
        out_shape=[jax.ShapeDtypeStruct((t, D_MODEL), F32), jax.ShapeDtypeStruct((t, D_MODEL), BF16),
                   jax.ShapeDtypeStruct((t, LANES), F32)],
        compiler_params=_params(1),
        name="outproj",
    )(x2d, olat, ohg, *wts)


def _moe_body(hn_ref, h1_ref, gate_ref, w1_ref, w3_ref, w2_ref, fin_ref, y_ref):
    t = hn_ref[...]
    gates = gate_ref[...]
    acc = jnp.zeros(h1_ref.shape, F32)
    for grp in range(N_GROUPS):
        parts = []
        for j in range(EXPERTS_PER_GROUP):
            e = grp * EXPERTS_PER_GROUP + j
            a = _dot(t, w1_ref[e])
            c = _dot(t, w3_ref[e])
            lane = GATE_LANE0 + e
            parts.append((a * jax.nn.sigmoid(a) * c * gates[:, lane:lane + 1]).astype(BF16))
        acc = acc + _dot(jnp.concatenate(parts, axis=-1), w2_ref[grp])
    y_ref[...] = _rms(h1_ref[...] + acc, fin_ref[...])


def _moe(hn, h1, gates, w1g, w3g, w2g, fin, tm):
    t = hn.shape[0]
    row = lambda w: pl.BlockSpec((tm, w), lambda i: (i, 0))
    resident = lambda a: pl.BlockSpec(a.shape, lambda i: (0,) * a.ndim, pipeline_mode=pl.Buffered(1))
    return pl.pallas_call(
        _moe_body,
        grid=(t // tm,),
        in_specs=[row(D_MODEL), row(D_MODEL), row(LANES), resident(w1g), resident(w3g), resident(w2g),
                  _const_spec(fin.shape)],
        out_specs=row(D_MODEL),
        out_shape=jax.ShapeDtypeStruct((t, D_MODEL), F32),
        compiler_params=_params(1),
        name="moe",
    )(hn, h1, gates, w1g, w3g, w2g, fin)


def _rope_tables(pos):
    half = ROPE_DIM // 2
    inv = ROPE_THETA ** (-jnp.arange(half, dtype=F32) / half)
    ang = pos.astype(F32)[:, None] * inv[None, :]
    cos, sin = jnp.cos(ang), jnp.sin(ang)
    n = pos.shape[0]
    z = lambda w: jnp.zeros((n, w), F32)
    cosk = jnp.concatenate([cos, cos, z(LANES - ROPE_DIM)], axis=1)
    sink = jnp.concatenate([-sin, sin, z(LANES - ROPE_DIM)], axis=1)
    cosq = jnp.concatenate([jnp.ones((n, QK_NOPE_DIM), F32), cos, cos, z(ROPE_DIM)], axis=1)
    sinq = jnp.concatenate([z(QK_NOPE_DIM), -sin, sin, z(ROPE_DIM)], axis=1)
    return cosk, sink, cosq, sinq


def kernel(x_prompt, x_sample, cache_ckv, cache_krope, state_hgrn, meta_tokens, attn_norm, w_in, q_norm, kv_norm, w_uq, w_uk, w_uv, hg_lb_logits, hg_gnorm, w_o, ffn_norm, w_router_group, b_router_group, w_router_expert, b_router_expert, w1, w3, w2, final_norm):
    bp, seq, _ = x_prompt.shape
    bs, dseq, _ = x_sample.shape
    half = ROPE_DIM // 2
    swap = np.concatenate([np.arange(half, ROPE_DIM), np.arange(half)])

    lb = jax.nn.softmax(hg_lb_logits.astype(F32), axis=0)[0][None, :]
    wi = w_in[0]
    o_kv, o_kr, o_hg = Q_LORA_RANK, Q_LORA_RANK + KV_LORA_RANK, Q_LORA_RANK + KV_LORA_RANK + ROPE_DIM
    w_kr = wi[:, o_kr:o_hg]
    win_ext = jnp.concatenate([wi[:, :o_kr], w_kr, w_kr[:, swap], jnp.zeros((D_MODEL, LANES - 2 * ROPE_DIM), F32),
                               wi[:, o_hg:]], axis=1).astype(BF16)
    wq = w_uq[0]
    wq_rope = wq[:, :, QK_NOPE_DIM:]
    wuq_ext = jnp.concatenate([wq, wq_rope[:, :, swap]], axis=2).reshape(Q_LORA_RANK, MLA_HEADS * LANES).astype(BF16)
    eye = jnp.eye(ROPE_DIM, dtype=F32)
    w2q = jnp.zeros((MLA_HEADS, LANES, KCAT), F32)
    w2q = w2q.at[:, :QK_NOPE_DIM, :KV_LORA_RANK].set(jnp.transpose(w_uk[0], (1, 2, 0)))
    w2q = w2q.at[:, QK_NOPE_DIM:QK_NOPE_DIM + ROPE_DIM, KV_LORA_RANK:KV_LORA_RANK + ROPE_DIM].set(eye[None])
    w2q = w2q.astype(BF16)
    wuv_bd = jnp.zeros((MLA_HEADS, KV_LORA_RANK, MLA_HEADS, V_HEAD_DIM), F32)
    for h in range(MLA_HEADS):
        wuv_bd = wuv_bd.at[h, :, h, :].set(w_uv[0][:, h, :])
    wuv_bd = wuv_bd.reshape(MLA_HEADS * KV_LORA_RANK, MLA_WIDTH).astype(BF16)
    wo = w_o[0].astype(BF16)
    wr = jnp.concatenate([w_router_group[0], w_router_expert[0],
                          jnp.zeros((D_MODEL, LANES - N_GROUPS - N_EXPERTS), F32)], axis=1)
    wr_hi = wr.astype(BF16)
    wr_lo = (wr - wr_hi.astype(F32)).astype(BF16)
    wr2 = jnp.concatenate([jnp.concatenate([wr_hi, wr_lo], axis=1),
                           jnp.concatenate([wr_hi, jnp.zeros_like(wr_lo)], axis=1)], axis=0)
    br = jnp.concatenate([b_router_group[0], b_router_expert[0],
                          jnp.zeros((LANES - N_GROUPS - N_EXPERTS,), F32)])[None, :].astype(F32)
    w1g = w1[0].astype(BF16)
    w3g = w3[0].astype(BF16)
    w2g = w2[0].astype(BF16).reshape(N_GROUPS, EXPERTS_PER_GROUP * EXPERT_FF, D_MODEL)
    in_wts = (attn_norm[0][None, :], win_ext, q_norm[0][None, :], kv_norm[0][None, :], wuq_ext, w2q)
    out_wts = (wuv_bd, wo, ffn_norm[0][None, :], wr2, br)
    gn = hg_gnorm[0][None, :]
    fin = final_norm[None, :]

    tm_p, tm_s = TILE_INPROJ
    tab_meta = _rope_tables(jnp.arange(N_META))
    tab_p = _rope_tables(N_META + jnp.arange(seq))
    tab_s = tuple(jnp.tile(a, (tm_s // dseq, 1)) for a in _rope_tables(PAST_LEN + jnp.arange(dseq)))
    ckv_m, kr_m, kcat_m, _, hz_m = _inproj(meta_tokens, tab_meta, in_wts, N_META)
    xp = x_prompt.reshape(bp * seq, D_MODEL)
    xs = x_sample.reshape(bs * dseq, D_MODEL)
    ckv_p, kr_p, kcat_p, qcat_p, hz_p = _inproj(xp, tab_p, in_wts, tm_p)
    ckv_s, kr_s, kcat_s, qcat_s, hz_s = _inproj(xs, tab_s, in_wts, tm_s)

    zero_state = jnp.zeros((1, HG_HEADS, HG_KEY_DIM, HG_VAL_DIM), F32)
    _, st_m = _hgrn(hz_m, lb, gn, zero_state, 1, N_META)
    ohg_p, st_p = _hgrn(hz_p, lb, gn, st_m, bp, TILE_HGRN, HGRN_TILES_PER_STEP)
    ohg_s, st_s = _hgrn(hz_s, lb, gn, state_hgrn[0].astype(F32), bs, dseq)

    kmeta = jnp.concatenate([kcat_m, jnp.zeros((LANES - N_META, KCAT), BF16)], axis=0)
    olat_p = _attn_prompt(qcat_p, kmeta, kcat_p, bp, ATTN_TQ, ATTN_TK)
    n_c = cache_ckv.shape[2]
    kcache = jnp.concatenate([cache_ckv[0], cache_krope[0], jnp.ones((bs, n_c, 1), F32),
                              jnp.zeros((bs, n_c, KCAT - ONE_COL - 1), F32)], axis=-1).astype(BF16)
    olat_s = _attn_sample(qcat_s, kcache, kcat_s, dseq, ATTN_SAMPLE_STREAMS)

    h1_p, hn_p, gate_p = _outproj(xp, olat_p, ohg_p, out_wts, TILE_OUTPROJ[0])
    h1_s, hn_s, gate_s = _outproj(xs, olat_s, ohg_s, out_wts, TILE_OUTPROJ[1])
    y_p = _moe(hn_p, h1_p, gate_p, w1g, w3g, w2g, fin, TILE_MOE[0])
    y_s = _moe(hn_s, h1_s, gate_s, w1g, w3g, w2g, fin, TILE_MOE[1])

    with_meta = lambda m, a, w: jnp.concatenate(
        [jnp.broadcast_to(m[None], (bp, N_META, w)), a.reshape(bp, seq, w)], axis=1)[None]
    return (y_p.reshape(bp, seq, D_MODEL), y_s.reshape(bs, dseq, D_MODEL),
            with_meta(ckv_m, ckv_p, KV_LORA_RANK), with_meta(kr_m, kr_p, ROPE_DIM), st_p[None],
            ckv_s.reshape(1, bs, dseq, KV_LORA_RANK), kr_s.reshape(1, bs, dseq, ROPE_DIM), st_s[None])
```

```python
import functools

import numpy as np
import jax
import jax.numpy as jnp
from jax import lax
from jax.experimental import pallas as pl
from jax.experimental.pallas import tpu as pltpu

F32 = jnp.float32
BF16 = jnp.bfloat16

D_MODEL = 1024
PAST_LEN = 2048
CHUNK = 64
N_META = 16
EPS = 1e-6
MLA_HEADS = 8
Q_LORA_RANK = 256
KV_LORA_RANK = 128
QK_NOPE_DIM = 64
ROPE_DIM = 32
V_HEAD_DIM = 64
ROPE_THETA = 10000.0
MLA_WIDTH = MLA_HEADS * V_HEAD_DIM
SM_SCALE = (QK_NOPE_DIM + ROPE_DIM) ** -0.5
LOG2E = 1.4426950408889634
Q_SCALE = SM_SCALE * LOG2E
HG_HEADS = 4
HG_KEY_DIM = 128
HG_VAL_DIM = 128
HG_WIDTH = HG_HEADS * HG_VAL_DIM
N_GROUPS = 4
EXPERTS_PER_GROUP = 4
N_EXPERTS = N_GROUPS * EXPERTS_PER_GROUP
EXPERT_FF = 256

LANES = 128
KCAT = 2 * LANES
ONE_COL = LANES + ROPE_DIM
NEG = -1e30
VMEM_LIMIT = 56 * 1024 * 1024
GATE_LANE0 = N_GROUPS
ATTN_UNROLL = 4

TILE_INPROJ = (512, 512)
TILE_HGRN = 128
HGRN_TILES_PER_STEP = 8
HGRN_SAMPLE_STREAMS = 4
ATTN_TQ, ATTN_TK = 128, 512
ATTN_SAMPLE_STREAMS = 4
TILE_OUTPROJ = (1024, 1024)
OUTPROJ_SLICE = 256
TILE_MOE = (512, 512)


def _rms(x, w):
    return x * lax.rsqrt(jnp.mean(x * x, axis=-1, keepdims=True) + EPS) * w


def _dot(a, b):
    return jnp.dot(a, b, preferred_element_type=F32)


def _dot_nt(a, b):
    return lax.dot_general(a, b, (((1,), (1,)), ((), ())), preferred_element_type=F32)


def _dot_tn(a, b):
    return lax.dot_general(a, b, (((0,), (0,)), ((), ())), preferred_element_type=F32)


def _split3(x):
    x1 = x.astype(BF16)
    r = x - x1.astype(F32)
    x2 = r.astype(BF16)
    x3 = (r - x2.astype(F32)).astype(BF16)
    return x1, x2, x3


def _params(n_axes):
    return pltpu.CompilerParams(dimension_semantics=("arbitrary",) * n_axes, vmem_limit_bytes=VMEM_LIMIT)


def _const_spec(shape):
    nd = len(shape)
    return pl.BlockSpec(shape, lambda *_: (0,) * nd)


def _inproj_body(x_ref, cosk_ref, sink_ref, cosq_ref, sinq_ref, an_ref, win_ref, qn_ref, kvn_ref, wuq_ref, w2_ref,
                 ckv_ref, krope_ref, kcat_ref, qcat_ref, hz_ref):
    x = x_ref[...]
    hn = _rms(x, an_ref[...]).astype(BF16)
    z = _dot(hn, win_ref[...])
    hz_ref[...] = z[:, 4 * LANES:]
    ckv = _rms(z[:, 2 * LANES:3 * LANES], kvn_ref[...])
    ckv_ref[...] = ckv
    kr = z[:, 3 * LANES:4 * LANES]
    rot = kr * cosk_ref[...] + pltpu.roll(kr, LANES - ROPE_DIM, 1) * sink_ref[...]
    krope_ref[...] = rot[:, :ROPE_DIM]
    lane = lax.broadcasted_iota(jnp.int32, rot.shape, 1)
    kcat_ref[...] = jnp.concatenate([ckv, jnp.where(lane == ROPE_DIM, 1.0, rot)], axis=-1).astype(BF16)
    cqn = _rms(z[:, :Q_LORA_RANK], qn_ref[...]).astype(BF16)
    qz = _dot(cqn, wuq_ref[...])
    cosq = cosq_ref[...]
    sinq = sinq_ref[...]
    for h in range(MLA_HEADS):
        blk = qz[:, h * LANES:(h + 1) * LANES]
        r = blk * cosq + pltpu.roll(blk, LANES - ROPE_DIM, 1) * sinq
        qcat_ref[h] = (_dot(r.astype(BF16), w2_ref[h]) * Q_SCALE).astype(BF16)


def _inproj(x2d, tabs, wts, tm):
    t = x2d.shape[0]
    assert t % tm == 0 and tabs[0].shape[0] % tm == 0, (t, tabs[0].shape, tm)
    period = tabs[0].shape[0] // tm
    grid = (t // tm,)
    row = lambda w: pl.BlockSpec((tm, w), lambda i: (i, 0))
    tab = pl.BlockSpec((tm, LANES), lambda i: (i % period, 0))
    an, win, qn, kvn, wuq, w2 = wts
    return pl.pallas_call(
        _inproj_body,
        grid=grid,
        in_specs=[row(D_MODEL), tab, tab, tab, tab, _const_spec(an.shape), _const_spec(win.shape),
                  _const_spec(qn.shape), _const_spec(kvn.shape), _const_spec(wuq.shape), _const_spec(w2.shape)],
        out_specs=[row(KV_LORA_RANK), row(ROPE_DIM), row(KCAT),
                   pl.BlockSpec((MLA_HEADS, tm, KCAT), lambda i: (0, i, 0)), row(4 * HG_WIDTH)],
        out_shape=[jax.ShapeDtypeStruct((t, KV_LORA_RANK), F32), jax.ShapeDtypeStruct((t, ROPE_DIM), F32),
                   jax.ShapeDtypeStruct((t, KCAT), BF16), jax.ShapeDtypeStruct((MLA_HEADS, t, KCAT), BF16),
                   jax.ShapeDtypeStruct((t, 4 * HG_WIDTH), F32)],
        compiler_params=_params(1),
        name="inproj",
    )(x2d, *tabs, an, win, qn, kvn, wuq, w2)


def _hgrn_body(hq_ref, hf_ref, hi_ref, hg_ref, lb_ref, gn_ref, s0_ref, tri_ref, lv_ref, cst_ref,
               o_ref, sout_ref, st_ref, *, tt, n_sub, tile_is_stream):
    step = pl.program_id(1)
    n_levels = tt.bit_length() - 1

    if not tile_is_stream:
        @pl.when(step == 0)
        def _():
            for h in range(HG_HEADS):
                st_ref[h] = s0_ref[0, h].T

    lb = lb_ref[...]
    gn = gn_ref[...]
    tri = tri_ref[...]
    lv = lv_ref[...]
    for sub in range(n_sub):
        rows = slice(sub * tt, (sub + 1) * tt)
        slot = sub * HG_HEADS if tile_is_stream else 0
        if tile_is_stream:
            for h in range(HG_HEADS):
                st_ref[slot + h] = s0_ref[sub, h].T
        hq = hq_ref[rows, :]
        q = hq * jax.nn.sigmoid(hq) * (HG_KEY_DIM ** -0.5)
        f = lb + (1.0 - lb) * jax.nn.sigmoid(hf_ref[rows, :])
        g = jnp.log(f) * LOG2E
        kk = 1.0 - f
        v = hi_ref[rows, :]
        g1, g2, g3 = _split3(g)
        bc = _dot(tri, g1) + _dot(tri, g2) + _dot(tri, g3)

        bl = bc[tt - 1:tt, :]
        qs = (q * jnp.exp2(bc)).astype(BF16)
        kf = (kk * jnp.exp2(bl - bc)).astype(BF16)
        vb = v.astype(BF16)
        dl = jnp.exp2(bl)
        hg = hg_ref[rows, :]
        for h in range(HG_HEADS):
            cols = slice(h * LANES, (h + 1) * LANES)
            qh, kh, gh, bch = q[:, cols], kk[:, cols], g[:, cols], bc[:, cols]
            pairs = [(qh, kh)]
            pairs.append((qh * jnp.exp2(gh + cst_ref[0]), kh * cst_ref[1]))
            if n_levels > 1:
                g_prev = pltpu.roll(gh, 1, 0)
                g_next = pltpu.roll(gh, tt - 1, 0)
                pairs.append((qh * jnp.exp2(gh + g_prev * cst_ref[2] + cst_ref[3]),
                              kh * jnp.exp2(g_next * cst_ref[4] + cst_ref[5])))
            for lvl in range(2, n_levels):
                grp = 2 << lvl
                mid = bch.reshape(tt // grp, grp, LANES)[:, grp // 2 - 1:grp // 2, :]
                d = bch - jnp.broadcast_to(mid, (tt // grp, grp, LANES)).reshape(tt, LANES)
                pairs.append((qh * jnp.exp2(d + cst_ref[2 + 2 * lvl]), kh * jnp.exp2(cst_ref[3 + 2 * lvl] - d)))
            att = jnp.zeros((tt, tt), F32)
            for idx, (qa, ka) in enumerate(pairs):
                att = jnp.where(lv == idx - 1, _dot_nt(qa.astype(BF16), ka.astype(BF16)), att)
            st = st_ref[slot + h]
            o = _dot(att.astype(BF16), vb[:, cols]) + _dot_nt(qs[:, cols], st.astype(BF16))
            st_ref[slot + h] = dl[:, cols] * st + _dot_tn(vb[:, cols], kf[:, cols])
            gate = hg[:, cols]
            o_ref[rows, cols] = (_rms(o, gn) * (gate * jax.nn.sigmoid(gate))).astype(BF16)
        if tile_is_stream:
            for h in range(HG_HEADS):
                sout_ref[sub, h] = st_ref[slot + h].T

    if not tile_is_stream:
        @pl.when(step == pl.num_programs(1) - 1)
        def _():
            for h in range(HG_HEADS):
                sout_ref[0, h] = st_ref[h].T


def _hgrn(hz, lb, gn, s0, batch, tt, n_sub=1, tile_is_stream=False):
    t_all = hz.shape[0]
    blk = tt * n_sub
    streams = n_sub if tile_is_stream else 1
    groups = batch // streams
    steps = t_all // groups // blk
    assert groups * streams == batch and groups * steps * blk == t_all and tt & (tt - 1) == 0, (t_all, batch, tt, n_sub)
    assert not tile_is_stream or steps == 1, (t_all, batch, tt, n_sub)
    idx = np.arange(tt)
    tri = jnp.asarray(idx[None, :] <= idx[:, None], BF16)
    diff = idx[:, None] ^ idx[None, :]
    level = np.where(diff > 0, np.floor(np.log2(np.maximum(diff, 1))), -1).astype(np.int32)
    lv = jnp.asarray(np.where(idx[None, :] <= idx[:, None], level, tt), jnp.int32)
    rows = np.broadcast_to(idx[:, None], (tt, LANES))
    bias = lambda keep: np.where(keep, 0.0, NEG)
    one = lambda keep: np.where(keep, 1.0, 0.0)
    cst = [bias(rows % 2 == 1), one(rows % 2 == 0),
           one(rows % 4 == 3), bias(rows % 4 >= 2), one(rows % 4 == 0), bias(rows % 4 < 2)]
    for lvl in range(2, tt.bit_length() - 1):
        upper = (rows >> lvl) % 2 == 1
        cst += [bias(upper), bias(~upper)]
    cst = jnp.asarray(np.stack(cst), F32)
    col = lambda c: pl.BlockSpec((blk, HG_WIDTH), lambda b, s: (b * steps + s, c))
    shared_s0 = s0.shape[0] == 1
    state = lambda shared: pl.BlockSpec((streams, HG_HEADS, HG_KEY_DIM, HG_VAL_DIM),
                                        (lambda b, s: (0, 0, 0, 0)) if shared else (lambda b, s: (b, 0, 0, 0)))
    return pl.pallas_call(
        functools.partial(_hgrn_body, tt=tt, n_sub=n_sub, tile_is_stream=tile_is_stream),
        grid=(groups, steps),
        in_specs=[col(0), col(1), col(2), col(3), _const_spec(lb.shape), _const_spec(gn.shape), state(shared_s0),
                  _const_spec(tri.shape), _const_spec(lv.shape), _const_spec(cst.shape)],
        out_specs=[pl.BlockSpec((blk, HG_WIDTH), lambda b, s: (b * steps + s, 0)), state(False)],
        out_shape=[jax.ShapeDtypeStruct((t_all, HG_WIDTH), BF16),
                   jax.ShapeDtypeStruct((batch, HG_HEADS, HG_KEY_DIM, HG_VAL_DIM), F32)],
        scratch_shapes=[pltpu.VMEM((streams * HG_HEADS, HG_VAL_DIM, HG_KEY_DIM), F32)],
        compiler_params=_params(2),
        name="hgrn",
    )(hz, hz, hz, hz, lb, gn, s0, tri, lv, cst)


def _lanes(x, n):
    return jnp.concatenate([x] * n, axis=1)


def _softmax_update(s, kt, m_scr, acc_scr):
    m_prev = m_scr[...]
    m_new = jnp.maximum(m_prev, jnp.max(s, axis=1, keepdims=True))
    alpha = jnp.exp2(m_prev - m_new)
    p = jnp.exp2(s - _lanes(m_new, s.shape[1] // LANES))
    acc_scr[...] = _lanes(alpha, KCAT // LANES) * acc_scr[...] + _dot(p.astype(BF16), kt)
    m_scr[...] = m_new


def _attn_finish(acc, o_ref, tq, row0=0):
    o = acc[:, :KV_LORA_RANK] / acc[:, ONE_COL:ONE_COL + 1]
    for h in range(MLA_HEADS):
        o_ref[row0:row0 + tq, h * LANES:(h + 1) * LANES] = o[h * tq:(h + 1) * tq].astype(BF16)


def _attn_prompt_body(q_ref, qn_ref, kmeta_ref, k_ref, o_ref, m_scr, acc_scr, sl_scr, sm_scr, sa_scr, sb_scr,
                      *, tq, tk):
    i = pl.program_id(1)
    rows = MLA_HEADS * tq
    n_tiles = k_ref.shape[0] // tk
    q = q_ref[...].reshape(rows, KCAT)
    kmeta = kmeta_ref[...]
    n = (i * tq) // tk

    def keys(t):
        return k_ref[pl.ds(pl.multiple_of(t * tk, tk), tk), :]

    def scores(qv, t, dst):
        dst[...] = _dot_nt(qv, keys(t))

    def consume(src, t):
        _softmax_update(src[...], keys(t), m_scr, acc_scr)

    def first_stage(qv, t):
        scores(qv, t, sl_scr)
        sm_scr[...] = _dot_nt(qv, kmeta)

    def prefetch():
        first_stage(qn_ref[...].reshape(rows, KCAT), jnp.minimum(((i + 1) * tq) // tk, n_tiles - 1))

    @pl.when(i == 0)
    def _():
        first_stage(q, 0)

    def stage0(width):
        scores(q, 0, sa_scr)
        limit = ((i * tq + lax.broadcasted_iota(jnp.int32, (tq, width), 0)) // CHUNK + 1) * CHUNK - n * tk
        bias = jnp.where(lax.broadcasted_iota(jnp.int32, (tq, width), 1) < limit, 0.0, NEG)
        s_last = sl_scr[:, :width] + jnp.concatenate([bias] * MLA_HEADS, axis=0)
        bias_meta = jnp.where(lax.broadcasted_iota(jnp.int32, (tq, LANES), 1) < N_META, 0.0, NEG)
        s_meta = sm_scr[...] + jnp.concatenate([bias_meta] * MLA_HEADS, axis=0)
        m0 = jnp.maximum(jnp.max(s_last, axis=1, keepdims=True), jnp.max(s_meta, axis=1, keepdims=True))
        m0 = jnp.broadcast_to(m0, (rows, LANES))
        p_last = jnp.exp2(s_last - _lanes(m0, width // LANES)).astype(BF16)
        p_meta = jnp.exp2(s_meta - m0).astype(BF16)
        k_last = k_ref[pl.ds(pl.multiple_of(n * tk, tk), width), :]
        acc_scr[...] = _dot(p_last, k_last) + _dot(p_meta, kmeta)
        m_scr[...] = m0

    own_end = (i + 1) * tq - n * tk

    @pl.when(own_end <= tk // 2)
    def _():
        stage0(tk // 2)

    @pl.when(own_end > tk // 2)
    def _():
        stage0(tk)

    bufs = (sa_scr, sb_scr)

    def run(first, count, last):
        for j in range(count):
            if j + 1 < count or not last:
                scores(q, first + j + 1, bufs[(j + 1) % 2])
            else:
                prefetch()
            consume(bufs[j % 2], first + j)

    def body(p, carry):
        run(ATTN_UNROLL * p, ATTN_UNROLL, False)
        return carry

    trips = jnp.maximum(n - 1, 0) // ATTN_UNROLL
    lax.fori_loop(0, trips, body, 0)
    base = ATTN_UNROLL * trips

    @pl.when(n == 0)
    def _():
        prefetch()

    for count in range(1, ATTN_UNROLL + 1):
        @pl.when(n - base == count)
        def _(count=count):
            run(base, count, True)

    _attn_finish(acc_scr[...], o_ref, tq)


def _attn_prompt(qcat, kmeta, kcat, batch, tq, tk):
    t_all = kcat.shape[0]
    seq = t_all // batch
    nq = seq // tq
    assert seq % tk == 0 and tk % tq == 0 and tq % CHUNK == 0 and ATTN_UNROLL % 2 == 0, (seq, tq, tk)
    rows = MLA_HEADS * tq
    last_block = t_all // tq - 1
    return pl.pallas_call(
        functools.partial(_attn_prompt_body, tq=tq, tk=tk),
        grid=(batch, nq),
        in_specs=[pl.BlockSpec((MLA_HEADS, tq, KCAT), lambda b, i: (0, b * nq + i, 0)),
                  pl.BlockSpec((MLA_HEADS, tq, KCAT), lambda b, i: (0, jnp.minimum(b * nq + i + 1, last_block), 0)),
                  _const_spec(kmeta.shape),
                  pl.BlockSpec((seq, KCAT), lambda b, i: (b, 0))],
        out_specs=pl.BlockSpec((tq, MLA_HEADS * LANES), lambda b, i: (b * nq + i, 0)),
        out_shape=jax.ShapeDtypeStruct((t_all, MLA_HEADS * LANES), BF16),
        scratch_shapes=[pltpu.VMEM((rows, LANES), F32), pltpu.VMEM((rows, KCAT), F32),
                        pltpu.VMEM((rows, tk), F32), pltpu.VMEM((rows, LANES), F32),
                        pltpu.VMEM((rows, tk), F32), pltpu.VMEM((rows, tk), F32)],
        compiler_params=_params(2),
        name="attn_prompt",
    )(qcat, qcat, kmeta, kcat)


def _attn_sample_body(q_ref, kc_ref, kn_ref, o_ref, *, tq, ns):
    rows = MLA_HEADS * tq
    for s in range(ns):
        q = q_ref[:, s * tq:(s + 1) * tq, :].reshape(rows, KCAT)
        kc = kc_ref[s]
        kn = kn_ref[s * tq:(s + 1) * tq, :]
        s_c = _dot_nt(q, kc)
        s_n = _dot_nt(q, kn)
        m = jnp.maximum(jnp.max(s_c, axis=1, keepdims=True), jnp.max(s_n, axis=1, keepdims=True))
        acc = _dot(jnp.exp2(s_c - m).astype(BF16), kc) + _dot(jnp.exp2(s_n - m).astype(BF16), kn)
        _attn_finish(acc, o_ref, tq, s * tq)


def _attn_sample(qcat, kcache, kcat, tq, ns):
    batch = kcache.shape[0]
    t_all = kcat.shape[0]
    assert batch % ns == 0 and t_all == batch * tq, (batch, ns, t_all, tq)
    return pl.pallas_call(
        functools.partial(_attn_sample_body, tq=tq, ns=ns),
        grid=(batch // ns,),
        in_specs=[pl.BlockSpec((MLA_HEADS, ns * tq, KCAT), lambda b: (0, b, 0)),
                  pl.BlockSpec((ns,) + kcache.shape[1:], lambda b: (b, 0, 0)),
                  pl.BlockSpec((ns * tq, KCAT), lambda b: (b, 0))],
        out_specs=pl.BlockSpec((ns * tq, MLA_HEADS * LANES), lambda b: (b, 0)),
        out_shape=jax.ShapeDtypeStruct((t_all, MLA_HEADS * LANES), BF16),
        compiler_params=_params(1),
        name="attn_sample",
    )(qcat, kcache, kcat)


def _outproj_body(x_ref, olat_ref, ohg_ref, wuv_ref, wo_ref, fn_ref, wr_ref, br_ref,
                  h1_ref, hn_ref, gate_ref, *, n_split):
    tm = x_ref.shape[0] // n_split
    for r in range(n_split):
        rows = slice(r * tm, (r + 1) * tm)
        _outproj_rows(x_ref.at[rows], olat_ref.at[rows], ohg_ref.at[rows], wuv_ref, wo_ref, fn_ref, wr_ref,
                      br_ref, h1_ref.at[rows], hn_ref.at[rows], gate_ref.at[rows])


def _outproj_rows(x_ref, olat_ref, ohg_ref, wuv_ref, wo_ref, fn_ref, wr_ref, br_ref, h1_ref, hn_ref, gate_ref):
    o_mla = _dot(olat_ref[...], wuv_ref[...]).astype(BF16)
    mix = _dot(o_mla, wo_ref[:MLA_WIDTH, :]) + _dot(ohg_ref[...], wo_ref[MLA_WIDTH:, :])
    h1 = x_ref[...] + mix
    h1_ref[...] = h1
    hn = _rms(h1, fn_ref[...])
    hn_hi = hn.astype(BF16)
    hn_ref[...] = hn_hi
    hn_lo = (hn - hn_hi.astype(F32)).astype(BF16)
    r2 = _dot(jnp.concatenate([hn_hi, hn_lo], axis=1), wr_ref[...])
    logits = r2[:, :LANES] + r2[:, LANES:] + br_ref[...]
    lane = lax.broadcasted_iota(jnp.int32, logits.shape, 1)
    gl = jnp.where(lane < N_GROUPS, logits, NEG)
    gmax = jnp.max(gl, axis=1, keepdims=True)
    g_top = 1.0 / jnp.sum(jnp.exp(gl - gmax), axis=1, keepdims=True)
    g_idx = jnp.min(jnp.where(gl == gmax, lane, LANES), axis=1, keepdims=True)
    lo = GATE_LANE0 + EXPERTS_PER_GROUP * g_idx
    el = jnp.where((lane >= lo) & (lane < lo + EXPERTS_PER_GROUP), logits, NEG)
    e1 = jnp.max(el, axis=1, keepdims=True)
    i1 = jnp.min(jnp.where(el == e1, lane, LANES), axis=1, keepdims=True)
    el2 = jnp.where(lane == i1, NEG, el)
    e2 = jnp.max(el2, axis=1, keepdims=True)
    i2 = jnp.min(jnp.where(el2 == e2, lane, LANES), axis=1, keepdims=True)
    esum = jnp.sum(jnp.exp(el - e1), axis=1, keepdims=True)
    p1 = 1.0 / esum
    p2 = jnp.exp(e2 - e1) / esum
    w1 = p1 / (p1 + p2) * g_top
    w2 = p2 / (p1 + p2) * g_top
    gate_ref[...] = jnp.where(lane == i1, w1, jnp.where(lane == i2, w2, 0.0))


def _outproj(x2d, olat, ohg, wts, tm):
    t = x2d.shape[0]
    assert t % tm == 0 and tm % OUTPROJ_SLICE == 0, (t, tm)
    row = lambda w: pl.BlockSpec((tm, w), lambda i: (i, 0))
    return pl.pallas_call(
        functools.partial(_outproj_body, n_split=tm // OUTPROJ_SLICE),
        grid=(t // tm,),
        in_specs=[row(D_MODEL), row(MLA_HEADS * LANES), row(HG_WIDTH)] + [_const_spec(w.shape) for w in wts],
        out_specs=[row(D_MODEL), row(D_MODEL), row(LANES)],
        out_shape=[jax.ShapeDtypeStruct((t, D_MODEL), F32), jax.ShapeDtypeStruct((t, D_MODEL), BF16),
                   jax.ShapeDtypeStruct((t, LANES), F32)],
        compiler_params=_params(1),
        name="outproj",
    )(x2d, olat, ohg, *wts)


def _moe_body(hn_ref, h1_ref, gate_ref, w1_ref, w3_ref, w2_ref, fin_ref, y_ref):
    t = hn_ref[...]
    gates = gate_ref[...]
    acc = jnp.zeros(h1_ref.shape, F32)
    for grp in range(N_GROUPS):
        parts = []
        for j in range(EXPERTS_PER_GROUP):
            e = grp * EXPERTS_PER_GROUP + j
            a = _dot(t, w1_ref[e])
            c = _dot(t, w3_ref[e])
            lane = GATE_LANE0 + e
            parts.append((a * jax.nn.sigmoid(a) * c * gates[:, lane:lane + 1]).astype(BF16))
        acc = acc + _dot(jnp.concatenate(parts, axis=-1), w2_ref[grp])
    y_ref[...] = _rms(h1_ref[...] + acc, fin_ref[...])


def _moe(hn, h1, gates, w1g, w3g, w2g, fin, tm):
    t = hn.shape[0]
    assert t % tm == 0, (t, tm)
    row = lambda w: pl.BlockSpec((tm, w), lambda i: (i, 0))
    resident = lambda a: pl.BlockSpec(a.shape, lambda i: (0,) * a.ndim, pipeline_mode=pl.Buffered(1))
    return pl.pallas_call(
        _moe_body,
        grid=(t // tm,),
        in_specs=[row(D_MODEL), row(D_MODEL), row(LANES), resident(w1g), resident(w3g), resident(w2g),
                  _const_spec(fin.shape)],
        out_specs=row(D_MODEL),
        out_shape=jax.ShapeDtypeStruct((t, D_MODEL), F32),
        compiler_params=_params(1),
        name="moe",
    )(hn, h1, gates, w1g, w3g, w2g, fin)


def _rope_tables(pos):
    half = ROPE_DIM // 2
    inv = ROPE_THETA ** (-jnp.arange(half, dtype=F32) / half)
    ang = pos.astype(F32)[:, None] * inv[None, :]
    cos, sin = jnp.cos(ang), jnp.sin(ang)
    n = pos.shape[0]
    z = lambda w: jnp.zeros((n, w), F32)
    cosk = jnp.concatenate([cos, cos, z(LANES - ROPE_DIM)], axis=1)
    sink = jnp.concatenate([-sin, sin, z(LANES - ROPE_DIM)], axis=1)
    cosq = jnp.concatenate([jnp.ones((n, QK_NOPE_DIM), F32), cos, cos, z(ROPE_DIM)], axis=1)
    sinq = jnp.concatenate([z(QK_NOPE_DIM), -sin, sin, z(ROPE_DIM)], axis=1)
    return cosk, sink, cosq, sinq


def kernel(x_prompt, x_sample, cache_ckv, cache_krope, state_hgrn, meta_tokens, attn_norm, w_in, q_norm, kv_norm, w_uq, w_uk, w_uv, hg_lb_logits, hg_gnorm, w_o, ffn_norm, w_router_group, b_router_group, w_router_expert, b_router_expert, w1, w3, w2, final_norm):
    bp, seq, _ = x_prompt.shape
    bs, dseq, _ = x_sample.shape
    half = ROPE_DIM // 2
    swap = np.concatenate([np.arange(half, ROPE_DIM), np.arange(half)])

    lb = jax.nn.softmax(hg_lb_logits.astype(F32), axis=0)[0][None, :]
    wi = w_in[0]
    o_kv, o_kr, o_hg = Q_LORA_RANK, Q_LORA_RANK + KV_LORA_RANK, Q_LORA_RANK + KV_LORA_RANK + ROPE_DIM
    w_kr = wi[:, o_kr:o_hg]
    win_ext = jnp.concatenate([wi[:, :o_kr], w_kr, w_kr[:, swap], jnp.zeros((D_MODEL, LANES - 2 * ROPE_DIM), F32),
                               wi[:, o_hg:]], axis=1).astype(BF16)
    wq = w_uq[0]
    wq_rope = wq[:, :, QK_NOPE_DIM:]
    wuq_ext = jnp.concatenate([wq, wq_rope[:, :, swap]], axis=2).reshape(Q_LORA_RANK, MLA_HEADS * LANES).astype(BF16)
    eye = jnp.eye(ROPE_DIM, dtype=F32)
    w2q = jnp.zeros((MLA_HEADS, LANES, KCAT), F32)
    w2q = w2q.at[:, :QK_NOPE_DIM, :KV_LORA_RANK].set(jnp.transpose(w_uk[0], (1, 2, 0)))
    w2q = w2q.at[:, QK_NOPE_DIM:QK_NOPE_DIM + ROPE_DIM, KV_LORA_RANK:KV_LORA_RANK + ROPE_DIM].set(eye[None])
    w2q = w2q.astype(BF16)
    wuv_bd = jnp.zeros((MLA_HEADS, KV_LORA_RANK, MLA_HEADS, V_HEAD_DIM), F32)
    for h in range(MLA_HEADS):
        wuv_bd = wuv_bd.at[h, :, h, :].set(w_uv[0][:, h, :])
    wuv_bd = wuv_bd.reshape(MLA_HEADS * KV_LORA_RANK, MLA_WIDTH).astype(BF16)
    wo = w_o[0].astype(BF16)
    wr = jnp.concatenate([w_router_group[0], w_router_expert[0],
                          jnp.zeros((D_MODEL, LANES - N_GROUPS - N_EXPERTS), F32)], axis=1)
    wr_hi = wr.astype(BF16)
    wr_lo = (wr - wr_hi.astype(F32)).astype(BF16)
    wr2 = jnp.concatenate([jnp.concatenate([wr_hi, wr_lo], axis=1),
                           jnp.concatenate([wr_hi, jnp.zeros_like(wr_lo)], axis=1)], axis=0)
    br = jnp.concatenate([b_router_group[0], b_router_expert[0],
                          jnp.zeros((LANES - N_GROUPS - N_EXPERTS,), F32)])[None, :].astype(F32)
    w1g = w1[0].astype(BF16)
    w3g = w3[0].astype(BF16)
    w2g = w2[0].astype(BF16).reshape(N_GROUPS, EXPERTS_PER_GROUP * EXPERT_FF, D_MODEL)
    in_wts = (attn_norm[0][None, :], win_ext, q_norm[0][None, :], kv_norm[0][None, :], wuq_ext, w2q)
    out_wts = (wuv_bd, wo, ffn_norm[0][None, :], wr2, br)
    gn = hg_gnorm[0][None, :]
    fin = final_norm[None, :]

    tm_p, tm_s = TILE_INPROJ
    tab_meta = _rope_tables(jnp.arange(N_META))
    tab_p = _rope_tables(N_META + jnp.arange(seq))
    tab_s = tuple(jnp.tile(a, (tm_s // dseq, 1)) for a in _rope_tables(PAST_LEN + jnp.arange(dseq)))
    ckv_m, kr_m, kcat_m, _, hz_m = _inproj(meta_tokens, tab_meta, in_wts, N_META)
    xp = x_prompt.reshape(bp * seq, D_MODEL)
    xs = x_sample.reshape(bs * dseq, D_MODEL)
    ckv_p, kr_p, kcat_p, qcat_p, hz_p = _inproj(xp, tab_p, in_wts, tm_p)
    ckv_s, kr_s, kcat_s, qcat_s, hz_s = _inproj(xs, tab_s, in_wts, tm_s)

    zero_state = jnp.zeros((1, HG_HEADS, HG_KEY_DIM, HG_VAL_DIM), F32)
    _, st_m = _hgrn(hz_m, lb, gn, zero_state, 1, N_META)
    ohg_p, st_p = _hgrn(hz_p, lb, gn, st_m, bp, TILE_HGRN, HGRN_TILES_PER_STEP)
    ohg_s, st_s = _hgrn(hz_s, lb, gn, state_hgrn[0].astype(F32), bs, dseq, HGRN_SAMPLE_STREAMS, True)

    kmeta = jnp.concatenate([kcat_m, jnp.zeros((LANES - N_META, KCAT), BF16)], axis=0)
    olat_p = _attn_prompt(qcat_p, kmeta, kcat_p, bp, ATTN_TQ, ATTN_TK)
    n_c = cache_ckv.shape[2]
    kcache = jnp.concatenate([cache_ckv[0], cache_krope[0], jnp.ones((bs, n_c, 1), F32),
                              jnp.zeros((bs, n_c, KCAT - ONE_COL - 1), F32)], axis=-1).astype(BF16)
    olat_s = _attn_sample(qcat_s, kcache, kcat_s, dseq, ATTN_SAMPLE_STREAMS)

    h1_p, hn_p, gate_p = _outproj(xp, olat_p, ohg_p, out_wts, TILE_OUTPROJ[0])
    h1_s, hn_s, gate_s = _outproj(xs, olat_s, ohg_s, out_wts, TILE_OUTPROJ[1])
    y_p = _moe(hn_p, h1_p, gate_p, w1g, w3g, w2g, fin, TILE_MOE[0])
    y_s = _moe(hn_s, h1_s, gate_s, w1g, w3g, w2g, fin, TILE_MOE[1])

    with_meta = lambda m, a, w: jnp.concatenate(
        [jnp.broadcast_to(m[None], (bp, N_META, w)), a.reshape(bp, seq, w)], axis=1)[None]
    return (y_p.reshape(bp, seq, D_MODEL), y_s.reshape(bs, dseq, D_MODEL),
            with_meta(ckv_m, ckv_p, KV_LORA_RANK), with_meta(kr_m, kr_p, ROPE_DIM), st_p[None],
            ckv_s.reshape(1, bs, dseq, KV_LORA_RANK), kr_s.reshape(1, bs, dseq, ROPE_DIM), st_s[None])
```

```python
import functools

import numpy as np
import jax
import jax.numpy as jnp
from jax import lax
from jax.experimental import pallas as pl
from jax.experimental.pallas import tpu as pltpu

F32 = jnp.float32
BF16 = jnp.bfloat16

D_MODEL = 1024
PAST_LEN = 2048
CHUNK = 64
N_META = 16
EPS = 1e-6
MLA_HEADS = 8
Q_LORA_RANK = 256
KV_LORA_RANK = 128
QK_NOPE_DIM = 64
ROPE_DIM = 32
V_HEAD_DIM = 64
ROPE_THETA = 10000.0
MLA_WIDTH = MLA_HEADS * V_HEAD_DIM
SM_SCALE = (QK_NOPE_DIM + ROPE_DIM) ** -0.5
LOG2E = 1.4426950408889634
Q_SCALE = SM_SCALE * LOG2E
HG_HEADS = 4
HG_KEY_DIM = 128
HG_VAL_DIM = 128
HG_WIDTH = HG_HEADS * HG_VAL_DIM
N_GROUPS = 4
EXPERTS_PER_GROUP = 4
N_EXPERTS = N_GROUPS * EXPERTS_PER_GROUP
EXPERT_FF = 256

LANES = 128
KCAT = 2 * LANES
ONE_COL = LANES + ROPE_DIM
NEG = -1e30
VMEM_LIMIT = 56 * 1024 * 1024
GATE_LANE0 = N_GROUPS
ATTN_UNROLL = 4

TILE_INPROJ = (512, 512)
TILE_HGRN = 128
HGRN_TILES_PER_STEP = 8
HGRN_SAMPLE_STREAMS = 4
ATTN_TQ, ATTN_TK = 128, 512
ATTN_SAMPLE_STREAMS = 4
TILE_OUTPROJ = (1024, 1024)
OUTPROJ_SLICE = 256
TILE_MOE = (512, 512)


def _rms(x, w):
    return x * lax.rsqrt(jnp.mean(x * x, axis=-1, keepdims=True) + EPS) * w


def _dot(a, b):
    return jnp.dot(a, b, preferred_element_type=F32)


def _dot_nt(a, b):
    return lax.dot_general(a, b, (((1,), (1,)), ((), ())), preferred_element_type=F32)


def _dot_tn(a, b):
    return lax.dot_general(a, b, (((0,), (0,)), ((), ())), preferred_element_type=F32)


def _split3(x):
    x1 = x.astype(BF16)
    r = x - x1.astype(F32)
    x2 = r.astype(BF16)
    x3 = (r - x2.astype(F32)).astype(BF16)
    return x1, x2, x3


def _params(n_axes):
    return pltpu.CompilerParams(dimension_semantics=("arbitrary",) * n_axes, vmem_limit_bytes=VMEM_LIMIT)


def _const_spec(shape):
    nd = len(shape)
    return pl.BlockSpec(shape, lambda *_: (0,) * nd)


def _inproj_body(x_ref, cosk_ref, sink_ref, cosq_ref, sinq_ref, an_ref, win_ref, qn_ref, kvn_ref, wuq_ref, w2_ref,
                 ckv_ref, krope_ref, kcat_ref, qcat_ref, hz_ref):
    x = x_ref[...]
    hn = _rms(x, an_ref[...]).astype(BF16)
    z = _dot(hn, win_ref[...])
    hz_ref[...] = z[:, 4 * LANES:]
    ckv = _rms(z[:, 2 * LANES:3 * LANES], kvn_ref[...])
    ckv_ref[...] = ckv
    kr = z[:, 3 * LANES:4 * LANES]
    rot = kr * cosk_ref[...] + pltpu.roll(kr, LANES - ROPE_DIM, 1) * sink_ref[...]
    krope_ref[...] = rot[:, :ROPE_DIM]
    lane = lax.broadcasted_iota(jnp.int32, rot.shape, 1)
    kcat_ref[...] = jnp.concatenate([ckv, jnp.where(lane == ROPE_DIM, 1.0, rot)], axis=-1).astype(BF16)
    cqn = _rms(z[:, :Q_LORA_RANK], qn_ref[...]).astype(BF16)
    qz = _dot(cqn, wuq_ref[...])
    cosq = cosq_ref[...]
    sinq = sinq_ref[...]
    for h in range(MLA_HEADS):
        blk = qz[:, h * LANES:(h + 1) * LANES]
        r = blk * cosq + pltpu.roll(blk, LANES - ROPE_DIM, 1) * sinq
        qcat_ref[h] = (_dot(r.astype(BF16), w2_ref[h]) * Q_SCALE).astype(BF16)


def _inproj(x2d, tabs, wts, tm):
    t = x2d.shape[0]
    assert t % tm == 0 and tabs[0].shape[0] % tm == 0, (t, tabs[0].shape, tm)
    period = tabs[0].shape[0] // tm
    grid = (t // tm,)
    row = lambda w: pl.BlockSpec((tm, w), lambda i: (i, 0))
    tab = pl.BlockSpec((tm, LANES), lambda i: (i % period, 0))
    an, win, qn, kvn, wuq, w2 = wts
    return pl.pallas_call(
        _inproj_body,
        grid=grid,
        in_specs=[row(D_MODEL), tab, tab, tab, tab, _const_spec(an.shape), _const_spec(win.shape),
                  _const_spec(qn.shape), _const_spec(kvn.shape), _const_spec(wuq.shape), _const_spec(w2.shape)],
        out_specs=[row(KV_LORA_RANK), row(ROPE_DIM), row(KCAT),
                   pl.BlockSpec((MLA_HEADS, tm, KCAT), lambda i: (0, i, 0)), row(4 * HG_WIDTH)],
        out_shape=[jax.ShapeDtypeStruct((t, KV_LORA_RANK), F32), jax.ShapeDtypeStruct((t, ROPE_DIM), F32),
                   jax.ShapeDtypeStruct((t, KCAT), BF16), jax.ShapeDtypeStruct((MLA_HEADS, t, KCAT), BF16),
                   jax.ShapeDtypeStruct((t, 4 * HG_WIDTH), F32)],
        compiler_params=_params(1),
        name="inproj",
    )(x2d, *tabs, an, win, qn, kvn, wuq, w2)


def _hgrn_body(hq_ref, hf_ref, hi_ref, hg_ref, lb_ref, gn_ref, s0_ref, tri_ref, lv_ref, cst_ref,
               o_ref, sout_ref, st_ref, *, tt, n_sub, tile_is_stream):
    step = pl.program_id(1)
    n_levels = tt.bit_length() - 1

    if not tile_is_stream:
        @pl.when(step == 0)
        def _():
            for h in range(HG_HEADS):
                st_ref[h] = s0_ref[0, h].T

    lb = lb_ref[...]
    gn = gn_ref[...]
    tri = tri_ref[...]
    lv = lv_ref[...]
    for sub in range(n_sub):
        rows = slice(sub * tt, (sub + 1) * tt)
        slot = sub * HG_HEADS if tile_is_stream else 0
        if tile_is_stream:
            for h in range(HG_HEADS):
                st_ref[slot + h] = s0_ref[sub, h].T
        for h in range(HG_HEADS):
            cols = slice(h * LANES, (h + 1) * LANES)
            hq = hq_ref[rows, cols]
            qh = hq * jax.nn.sigmoid(hq) * (HG_KEY_DIM ** -0.5)
            f = lb[:, cols] + (1.0 - lb[:, cols]) * jax.nn.sigmoid(hf_ref[rows, cols])
            gh = jnp.log(f) * LOG2E
            kh = 1.0 - f
            g1, g2, g3 = _split3(gh)
            bch = _dot(tri, g1) + _dot(tri, g2) + _dot(tri, g3)
            bl = bch[tt - 1:tt, :]
            qs = (qh * jnp.exp2(bch)).astype(BF16)
            kf = (kh * jnp.exp2(bl - bch)).astype(BF16)
            vb = hi_ref[rows, cols].astype(BF16)
            dl = jnp.exp2(bl)
            pairs = [(qh, kh)]
            pairs.append((qh * jnp.exp2(gh + cst_ref[0]), kh * cst_ref[1]))
            if n_levels > 1:
                g_prev = pltpu.roll(gh, 1, 0)
                g_next = pltpu.roll(gh, tt - 1, 0)
                pairs.append((qh * jnp.exp2(gh + g_prev * cst_ref[2] + cst_ref[3]),
                              kh * jnp.exp2(g_next * cst_ref[4] + cst_ref[5])))
            for lvl in range(2, n_levels):
                grp = 2 << lvl
                mid = bch.reshape(tt // grp, grp, LANES)[:, grp // 2 - 1:grp // 2, :]
                d = bch - jnp.broadcast_to(mid, (tt // grp, grp, LANES)).reshape(tt, LANES)
                pairs.append((qh * jnp.exp2(d + cst_ref[2 + 2 * lvl]), kh * jnp.exp2(cst_ref[3 + 2 * lvl] - d)))
            att = jnp.zeros((tt, tt), F32)
            for idx, (qa, ka) in enumerate(pairs):
                att = jnp.where(lv == idx - 1, _dot_nt(qa.astype(BF16), ka.astype(BF16)), att)
            st = st_ref[slot + h]
            o = _dot(att.astype(BF16), vb) + _dot_nt(qs, st.astype(BF16))
            st_ref[slot + h] = dl * st + _dot_tn(vb, kf)
            gate = hg_ref[rows, cols]
            o_ref[rows, cols] = (_rms(o, gn) * (gate * jax.nn.sigmoid(gate))).astype(BF16)
        if tile_is_stream:
            for h in range(HG_HEADS):
                sout_ref[sub, h] = st_ref[slot + h].T

    if not tile_is_stream:
        @pl.when(step == pl.num_programs(1) - 1)
        def _():
            for h in range(HG_HEADS):
                sout_ref[0, h] = st_ref[h].T


def _hgrn(hz, lb, gn, s0, batch, tt, n_sub=1, tile_is_stream=False):
    t_all = hz.shape[0]
    blk = tt * n_sub
    streams = n_sub if tile_is_stream else 1
    groups = batch // streams
    steps = t_all // groups // blk
    assert groups * streams == batch and groups * steps * blk == t_all and tt & (tt - 1) == 0, (t_all, batch, tt, n_sub)
    assert not tile_is_stream or steps == 1, (t_all, batch, tt, n_sub)
    idx = np.arange(tt)
    tri = jnp.asarray(idx[None, :] <= idx[:, None], BF16)
    diff = idx[:, None] ^ idx[None, :]
    level = np.where(diff > 0, np.floor(np.log2(np.maximum(diff, 1))), -1).astype(np.int32)
    lv = jnp.asarray(np.where(idx[None, :] <= idx[:, None], level, tt), jnp.int32)
    rows = np.broadcast_to(idx[:, None], (tt, LANES))
    bias = lambda keep: np.where(keep, 0.0, NEG)
    one = lambda keep: np.where(keep, 1.0, 0.0)
    cst = [bias(rows % 2 == 1), one(rows % 2 == 0),
           one(rows % 4 == 3), bias(rows % 4 >= 2), one(rows % 4 == 0), bias(rows % 4 < 2)]
    for lvl in range(2, tt.bit_length() - 1):
        upper = (rows >> lvl) % 2 == 1
        cst += [bias(upper), bias(~upper)]
    cst = jnp.asarray(np.stack(cst), F32)
    col = lambda c: pl.BlockSpec((blk, HG_WIDTH), lambda b, s: (b * steps + s, c))
    shared_s0 = s0.shape[0] == 1
    state = lambda shared: pl.BlockSpec((streams, HG_HEADS, HG_KEY_DIM, HG_VAL_DIM),
                                        (lambda b, s: (0, 0, 0, 0)) if shared else (lambda b, s: (b, 0, 0, 0)))
    return pl.pallas_call(
        functools.partial(_hgrn_body, tt=tt, n_sub=n_sub, tile_is_stream=tile_is_stream),
        grid=(groups, steps),
        in_specs=[col(0), col(1), col(2), col(3), _const_spec(lb.shape), _const_spec(gn.shape), state(shared_s0),
                  _const_spec(tri.shape), _const_spec(lv.shape), _const_spec(cst.shape)],
        out_specs=[pl.BlockSpec((blk, HG_WIDTH), lambda b, s: (b * steps + s, 0)), state(False)],
        out_shape=[jax.ShapeDtypeStruct((t_all, HG_WIDTH), BF16),
                   jax.ShapeDtypeStruct((batch, HG_HEADS, HG_KEY_DIM, HG_VAL_DIM), F32)],
        scratch_shapes=[pltpu.VMEM((streams * HG_HEADS, HG_VAL_DIM, HG_KEY_DIM), F32)],
        compiler_params=_params(2),
        name="hgrn",
    )(hz, hz, hz, hz, lb, gn, s0, tri, lv, cst)


def _lanes(x, n):
    return jnp.concatenate([x] * n, axis=1)


def _softmax_update(s, kt, m_scr, acc_scr):
    m_prev = m_scr[...]
    m_new = jnp.maximum(m_prev, jnp.max(s, axis=1, keepdims=True))
    alpha = jnp.exp2(m_prev - m_new)
    p = jnp.exp2(s - _lanes(m_new, s.shape[1] // LANES))
    acc_scr[...] = _lanes(alpha, KCAT // LANES) * acc_scr[...] + _dot(p.astype(BF16), kt)
    m_scr[...] = m_new


def _attn_finish(acc, o_ref, tq, row0=0):
    o = acc[:, :KV_LORA_RANK] / acc[:, ONE_COL:ONE_COL + 1]
    for h in range(MLA_HEADS):
        o_ref[row0:row0 + tq, h * LANES:(h + 1) * LANES] = o[h * tq:(h + 1) * tq].astype(BF16)


def _attn_prompt_body(q_ref, qn_ref, kmeta_ref, k_ref, o_ref, m_scr, acc_scr, sl_scr, sm_scr, sa_scr, sb_scr,
                      *, tq, tk):
    i = pl.program_id(1)
    rows = MLA_HEADS * tq
    n_tiles = k_ref.shape[0] // tk
    q = q_ref[...].reshape(rows, KCAT)
    kmeta = kmeta_ref[...]
    n = (i * tq) // tk

    def keys(t):
        return k_ref[pl.ds(pl.multiple_of(t * tk, tk), tk), :]

    def scores(qv, t, dst):
        dst[...] = _dot_nt(qv, keys(t))

    def consume(src, t):
        _softmax_update(src[...], keys(t), m_scr, acc_scr)

    def first_stage(qv, t):
        scores(qv, t, sl_scr)
        sm_scr[...] = _dot_nt(qv, kmeta)

    def prefetch():
        first_stage(qn_ref[...].reshape(rows, KCAT), jnp.minimum(((i + 1) * tq) // tk, n_tiles - 1))

    @pl.when(i == 0)
    def _():
        first_stage(q, 0)

    def stage0(width):
        scores(q, 0, sa_scr)
        limit = ((i * tq + lax.broadcasted_iota(jnp.int32, (tq, width), 0)) // CHUNK + 1) * CHUNK - n * tk
        bias = jnp.where(lax.broadcasted_iota(jnp.int32, (tq, width), 1) < limit, 0.0, NEG)
        s_last = sl_scr[:, :width] + jnp.concatenate([bias] * MLA_HEADS, axis=0)
        bias_meta = jnp.where(lax.broadcasted_iota(jnp.int32, (tq, LANES), 1) < N_META, 0.0, NEG)
        s_meta = sm_scr[...] + jnp.concatenate([bias_meta] * MLA_HEADS, axis=0)
        m0 = jnp.maximum(jnp.max(s_last, axis=1, keepdims=True), jnp.max(s_meta, axis=1, keepdims=True))
        m0 = jnp.broadcast_to(m0, (rows, LANES))
        p_last = jnp.exp2(s_last - _lanes(m0, width // LANES)).astype(BF16)
        p_meta = jnp.exp2(s_meta - m0).astype(BF16)
        k_last = k_ref[pl.ds(pl.multiple_of(n * tk, tk), width), :]
        acc_scr[...] = _dot(p_last, k_last) + _dot(p_meta, kmeta)
        m_scr[...] = m0

    own_end = (i + 1) * tq - n * tk

    @pl.when(own_end <= tk // 2)
    def _():
        stage0(tk // 2)

    @pl.when(own_end > tk // 2)
    def _():
        stage0(tk)

    bufs = (sa_scr, sb_scr)

    def run(first, count, last):
        for j in range(count):
            if j + 1 < count or not last:
                scores(q, first + j + 1, bufs[(j + 1) % 2])
            else:
                prefetch()
            consume(bufs[j % 2], first + j)

    def body(p, carry):
        run(ATTN_UNROLL * p, ATTN_UNROLL, False)
        return carry

    trips = jnp.maximum(n - 1, 0) // ATTN_UNROLL
    lax.fori_loop(0, trips, body, 0)
    base = ATTN_UNROLL * trips

    @pl.when(n == 0)
    def _():
        prefetch()

    for count in range(1, ATTN_UNROLL + 1):
        @pl.when(n - base == count)
        def _(count=count):
            run(base, count, True)

    _attn_finish(acc_scr[...], o_ref, tq)


def _attn_prompt(qcat, kmeta, kcat, batch, tq, tk):
    t_all = kcat.shape[0]
    seq = t_all // batch
    nq = seq // tq
    assert seq % tk == 0 and tk % tq == 0 and tq % CHUNK == 0 and ATTN_UNROLL % 2 == 0, (seq, tq, tk)
    rows = MLA_HEADS * tq
    last_block = t_all // tq - 1
    return pl.pallas_call(
        functools.partial(_attn_prompt_body, tq=tq, tk=tk),
        grid=(batch, nq),
        in_specs=[pl.BlockSpec((MLA_HEADS, tq, KCAT), lambda b, i: (0, b * nq + i, 0)),
                  pl.BlockSpec((MLA_HEADS, tq, KCAT), lambda b, i: (0, jnp.minimum(b * nq + i + 1, last_block), 0)),
                  _const_spec(kmeta.shape),
                  pl.BlockSpec((seq, KCAT), lambda b, i: (b, 0))],
        out_specs=pl.BlockSpec((tq, MLA_HEADS * LANES), lambda b, i: (b * nq + i, 0)),
        out_shape=jax.ShapeDtypeStruct((t_all, MLA_HEADS * LANES), BF16),
        scratch_shapes=[pltpu.VMEM((rows, LANES), F32), pltpu.VMEM((rows, KCAT), F32),
                        pltpu.VMEM((rows, tk), F32), pltpu.VMEM((rows, LANES), F32),
                        pltpu.VMEM((rows, tk), F32), pltpu.VMEM((rows, tk), F32)],
        compiler_params=_params(2),
        name="attn_prompt",
    )(qcat, qcat, kmeta, kcat)


def _attn_sample_body(q_ref, kc_ref, kn_ref, o_ref, *, tq, ns):
    rows = MLA_HEADS * tq
    for s in range(ns):
        q = q_ref[:, s * tq:(s + 1) * tq, :].reshape(rows, KCAT)
        kc = kc_ref[s]
        kn = kn_ref[s * tq:(s + 1) * tq, :]
        s_c = _dot_nt(q, kc)
        s_n = _dot_nt(q, kn)
        m = jnp.maximum(jnp.max(s_c, axis=1, keepdims=True), jnp.max(s_n, axis=1, keepdims=True))
        acc = _dot(jnp.exp2(s_c - m).astype(BF16), kc) + _dot(jnp.exp2(s_n - m).astype(BF16), kn)
        _attn_finish(acc, o_ref, tq, s * tq)


def _attn_sample(qcat, kcache, kcat, tq, ns):
    batch = kcache.shape[0]
    t_all = kcat.shape[0]
    assert batch % ns == 0 and t_all == batch * tq, (batch, ns, t_all, tq)
    return pl.pallas_call(
        functools.partial(_attn_sample_body, tq=tq, ns=ns),
        grid=(batch // ns,),
        in_specs=[pl.BlockSpec((MLA_HEADS, ns * tq, KCAT), lambda b: (0, b, 0)),
                  pl.BlockSpec((ns,) + kcache.shape[1:], lambda b: (b, 0, 0)),
                  pl.BlockSpec((ns * tq, KCAT), lambda b: (b, 0))],
        out_specs=pl.BlockSpec((ns * tq, MLA_HEADS * LANES), lambda b: (b, 0)),
        out_shape=jax.ShapeDtypeStruct((t_all, MLA_HEADS * LANES), BF16),
        compiler_params=_params(1),
        name="attn_sample",
    )(qcat, kcache, kcat)


def _outproj_body(x_ref, olat_ref, ohg_ref, wuv_ref, wo_ref, fn_ref, wr_ref, br_ref,
                  h1_ref, hn_ref, gate_ref, *, n_split):
    tm = x_ref.shape[0] // n_split
    for r in range(n_split):
        rows = slice(r * tm, (r + 1) * tm)
        _outproj_rows(x_ref.at[rows], olat_ref.at[rows], ohg_ref.at[rows], wuv_ref, wo_ref, fn_ref, wr_ref,
                      br_ref, h1_ref.at[rows], hn_ref.at[rows], gate_ref.at[rows])


def _outproj_rows(x_ref, olat_ref, ohg_ref, wuv_ref, wo_ref, fn_ref, wr_ref, br_ref, h1_ref, hn_ref, gate_ref):
    o_mla = _dot(olat_ref[...], wuv_ref[...]).astype(BF16)
    mix = _dot(o_mla, wo_ref[:MLA_WIDTH, :]) + _dot(ohg_ref[...], wo_ref[MLA_WIDTH:, :])
    h1 = x_ref[...] + mix
    h1_ref[...] = h1
    hn = _rms(h1, fn_ref[...])
    hn_hi = hn.astype(BF16)
    hn_ref[...] = hn_hi
    hn_lo = (hn - hn_hi.astype(F32)).astype(BF16)
    r2 = _dot(jnp.concatenate([hn_hi, hn_lo], axis=1), wr_ref[...])
    logits = r2[:, :LANES] + r2[:, LANES:] + br_ref[...]
    lane = lax.broadcasted_iota(jnp.int32, logits.shape, 1)
    gl = jnp.where(lane < N_GROUPS, logits, NEG)
    gmax = jnp.max(gl, axis=1, keepdims=True)
    g_top = 1.0 / jnp.sum(jnp.exp(gl - gmax), axis=1, keepdims=True)
    g_idx = jnp.min(jnp.where(gl == gmax, lane, LANES), axis=1, keepdims=True)
    lo = GATE_LANE0 + EXPERTS_PER_GROUP * g_idx
    el = jnp.where((lane >= lo) & (lane < lo + EXPERTS_PER_GROUP), logits, NEG)
    e1 = jnp.max(el, axis=1, keepdims=True)
    i1 = jnp.min(jnp.where(el == e1, lane, LANES), axis=1, keepdims=True)
    el2 = jnp.where(lane == i1, NEG, el)
    e2 = jnp.max(el2, axis=1, keepdims=True)
    i2 = jnp.min(jnp.where(el2 == e2, lane, LANES), axis=1, keepdims=True)
    esum = jnp.sum(jnp.exp(el - e1), axis=1, keepdims=True)
    p1 = 1.0 / esum
    p2 = jnp.exp(e2 - e1) / esum
    w1 = p1 / (p1 + p2) * g_top
    w2 = p2 / (p1 + p2) * g_top
    gate_ref[...] = jnp.where(lane == i1, w1, jnp.where(lane == i2, w2, 0.0))


def _outproj(x2d, olat, ohg, wts, tm):
    t = x2d.shape[0]
    assert t % tm == 0 and tm % OUTPROJ_SLICE == 0, (t, tm)
    row = lambda w: pl.BlockSpec((tm, w), lambda i: (i, 0))
    return pl.pallas_call(
        functools.partial(_outproj_body, n_split=tm // OUTPROJ_SLICE),
        grid=(t // tm,),
        in_specs=[row(D_MODEL), row(MLA_HEADS * LANES), row(HG_WIDTH)] + [_const_spec(w.shape) for w in wts],
        out_specs=[row(D_MODEL), row(D_MODEL), row(LANES)],
        out_shape=[jax.ShapeDtypeStruct((t, D_MODEL), F32), jax.ShapeDtypeStruct((t, D_MODEL), BF16),
                   jax.ShapeDtypeStruct((t, LANES), F32)],
        compiler_params=_params(1),
        name="outproj",
    )(x2d, olat, ohg, *wts)


def _moe_body(hn_ref, h1_ref, gate_ref, w1_ref, w3_ref, w2_ref, fin_ref, y_ref):
    t = hn_ref[...]
    gates = gate_ref[...]
    acc = jnp.zeros(h1_ref.shape, F32)
    for grp in range(N_GROUPS):
        parts = []
        for j in range(EXPERTS_PER_GROUP):
            e = grp * EXPERTS_PER_GROUP + j
            a = _dot(t, w1_ref[e])
            c = _dot(t, w3_ref[e])
            lane = GATE_LANE0 + e
            parts.append((a * jax.nn.sigmoid(a) * c * gates[:, lane:lane + 1]).astype(BF16))
        acc = acc + _dot(jnp.concatenate(parts, axis=-1), w2_ref[grp])
    y_ref[...] = _rms(h1_ref[...] + acc, fin_ref[...])


def _moe(hn, h1, gates, w1g, w3g, w2g, fin, tm):
    t = hn.shape[0]
    assert t % tm == 0, (t, tm)
    row = lambda w: pl.BlockSpec((tm, w), lambda i: (i, 0))
    resident = lambda a: pl.BlockSpec(a.shape, lambda i: (0,) * a.ndim, pipeline_mode=pl.Buffered(1))
    return pl.pallas_call(
        _moe_body,
        grid=(t // tm,),
        in_specs=[row(D_MODEL), row(D_MODEL), row(LANES), resident(w1g), resident(w3g), resident(w2g),
                  _const_spec(fin.shape)],
        out_specs=row(D_MODEL),
        out_shape=jax.ShapeDtypeStruct((t, D_MODEL), F32),
        compiler_params=_params(1),
        name="moe",
    )(hn, h1, gates, w1g, w3g, w2g, fin)


def _rope_tables(pos):
    half = ROPE_DIM // 2
    inv = ROPE_THETA ** (-jnp.arange(half, dtype=F32) / half)
    ang = pos.astype(F32)[:, None] * inv[None, :]
    cos, sin = jnp.cos(ang), jnp.sin(ang)
    n = pos.shape[0]
    z = lambda w: jnp.zeros((n, w), F32)
    cosk = jnp.concatenate([cos, cos, z(LANES - ROPE_DIM)], axis=1)
    sink = jnp.concatenate([-sin, sin, z(LANES - ROPE_DIM)], axis=1)
    cosq = jnp.concatenate([jnp.ones((n, QK_NOPE_DIM), F32), cos, cos, z(ROPE_DIM)], axis=1)
    sinq = jnp.concatenate([z(QK_NOPE_DIM), -sin, sin, z(ROPE_DIM)], axis=1)
    return cosk, sink, cosq, sinq


def kernel(x_prompt, x_sample, cache_ckv, cache_krope, state_hgrn, meta_tokens, attn_norm, w_in, q_norm, kv_norm, w_uq, w_uk, w_uv, hg_lb_logits, hg_gnorm, w_o, ffn_norm, w_router_group, b_router_group, w_router_expert, b_router_expert, w1, w3, w2, final_norm):
    bp, seq, _ = x_prompt.shape
    bs, dseq, _ = x_sample.shape
    half = ROPE_DIM // 2
    swap = np.concatenate([np.arange(half, ROPE_DIM), np.arange(half)])

    lb = jax.nn.softmax(hg_lb_logits.astype(F32), axis=0)[0][None, :]
    wi = w_in[0]
    o_kv, o_kr, o_hg = Q_LORA_RANK, Q_LORA_RANK + KV_LORA_RANK, Q_LORA_RANK + KV_LORA_RANK + ROPE_DIM
    w_kr = wi[:, o_kr:o_hg]
    win_ext = jnp.concatenate([wi[:, :o_kr], w_kr, w_kr[:, swap], jnp.zeros((D_MODEL, LANES - 2 * ROPE_DIM), F32),
                               wi[:, o_hg:]], axis=1).astype(BF16)
    wq = w_uq[0]
    wq_rope = wq[:, :, QK_NOPE_DIM:]
    wuq_ext = jnp.concatenate([wq, wq_rope[:, :, swap]], axis=2).reshape(Q_LORA_RANK, MLA_HEADS * LANES).astype(BF16)
    eye = jnp.eye(ROPE_DIM, dtype=F32)
    w2q = jnp.zeros((MLA_HEADS, LANES, KCAT), F32)
    w2q = w2q.at[:, :QK_NOPE_DIM, :KV_LORA_RANK].set(jnp.transpose(w_uk[0], (1, 2, 0)))
    w2q = w2q.at[:, QK_NOPE_DIM:QK_NOPE_DIM + ROPE_DIM, KV_LORA_RANK:KV_LORA_RANK + ROPE_DIM].set(eye[None])
    w2q = w2q.astype(BF16)
    wuv_bd = jnp.zeros((MLA_HEADS, KV_LORA_RANK, MLA_HEADS, V_HEAD_DIM), F32)
    for h in range(MLA_HEADS):
        wuv_bd = wuv_bd.at[h, :, h, :].set(w_uv[0][:, h, :])
    wuv_bd = wuv_bd.reshape(MLA_HEADS * KV_LORA_RANK, MLA_WIDTH).astype(BF16)
    wo = w_o[0].astype(BF16)
    wr = jnp.concatenate([w_router_group[0], w_router_expert[0],
                          jnp.zeros((D_MODEL, LANES - N_GROUPS - N_EXPERTS), F32)], axis=1)
    wr_hi = wr.astype(BF16)
    wr_lo = (wr - wr_hi.astype(F32)).astype(BF16)
    wr2 = jnp.concatenate([jnp.concatenate([wr_hi, wr_lo], axis=1),
                           jnp.concatenate([wr_hi, jnp.zeros_like(wr_lo)], axis=1)], axis=0)
    br = jnp.concatenate([b_router_group[0], b_router_expert[0],
                          jnp.zeros((LANES - N_GROUPS - N_EXPERTS,), F32)])[None, :].astype(F32)
    w1g = w1[0].astype(BF16)
    w3g = w3[0].astype(BF16)
    w2g = w2[0].astype(BF16).reshape(N_GROUPS, EXPERTS_PER_GROUP * EXPERT_FF, D_MODEL)
    in_wts = (attn_norm[0][None, :], win_ext, q_norm[0][None, :], kv_norm[0][None, :], wuq_ext, w2q)
    out_wts = (wuv_bd, wo, ffn_norm[0][None, :], wr2, br)
    gn = hg_gnorm[0][None, :]
    fin = final_norm[None, :]

    tm_p, tm_s = TILE_INPROJ
    tab_meta = _rope_tables(jnp.arange(N_META))
    tab_p = _rope_tables(N_META + jnp.arange(seq))
    tab_s = tuple(jnp.tile(a, (tm_s // dseq, 1)) for a in _rope_tables(PAST_LEN + jnp.arange(dseq)))
    ckv_m, kr_m, kcat_m, _, hz_m = _inproj(meta_tokens, tab_meta, in_wts, N_META)
    xp = x_prompt.reshape(bp * seq, D_MODEL)
    xs = x_sample.reshape(bs * dseq, D_MODEL)
    ckv_p, kr_p, kcat_p, qcat_p, hz_p = _inproj(xp, tab_p, in_wts, tm_p)
    ckv_s, kr_s, kcat_s, qcat_s, hz_s = _inproj(xs, tab_s, in_wts, tm_s)

    zero_state = jnp.zeros((1, HG_HEADS, HG_KEY_DIM, HG_VAL_DIM), F32)
    _, st_m = _hgrn(hz_m, lb, gn, zero_state, 1, N_META)
    ohg_p, st_p = _hgrn(hz_p, lb, gn, st_m, bp, TILE_HGRN, HGRN_TILES_PER_STEP)
    ohg_s, st_s = _hgrn(hz_s, lb, gn, state_hgrn[0].astype(F32), bs, dseq, HGRN_SAMPLE_STREAMS, True)

    kmeta = jnp.concatenate([kcat_m, jnp.zeros((LANES - N_META, KCAT), BF16)], axis=0)
    olat_p = _attn_prompt(qcat_p, kmeta, kcat_p, bp, ATTN_TQ, ATTN_TK)
    n_c = cache_ckv.shape[2]
    kcache = jnp.concatenate([cache_ckv[0], cache_krope[0], jnp.ones((bs, n_c, 1), F32),
                              jnp.zeros((bs, n_c, KCAT - ONE_COL - 1), F32)], axis=-1).astype(BF16)
    olat_s = _attn_sample(qcat_s, kcache, kcat_s, dseq, ATTN_SAMPLE_STREAMS)

    h1_p, hn_p, gate_p = _outproj(xp, olat_p, ohg_p, out_wts, TILE_OUTPROJ[0])
    h1_s, hn_s, gate_s = _outproj(xs, olat_s, ohg_s, out_wts, TILE_OUTPROJ[1])
    y_p = _moe(hn_p, h1_p, gate_p, w1g, w3g, w2g, fin, TILE_MOE[0])
    y_s = _moe(hn_s, h1_s, gate_s, w1g, w3g, w2g, fin, TILE_MOE[1])

    with_meta = lambda m, a, w: jnp.concatenate(
        [jnp.broadcast_to(m[None], (bp, N_META, w)), a.reshape(bp, seq, w)], axis=1)[None]
    return (y_p.reshape(bp, seq, D_MODEL), y_s.reshape(bs, dseq, D_MODEL),
            with_meta(ckv_m, ckv_p, KV_LORA_RANK), with_meta(kr_m, kr_p, ROPE_DIM), st_p[None],
            ckv_s.reshape(1, bs, dseq, KV_LORA_RANK), kr_s.reshape(1, bs, dseq, ROPE_DIM), st_s[None])
```
